```python
import math
import jax, jax.numpy as jnp
from jax import lax
import numpy as np

D_MODEL = 1024
BATCH = 16
SEQ = 4096
DEPTH = 1

S5_WIDTH = D_MODEL // 2
S5_GROUP = 16
S5_GROUPS = S5_WIDTH // S5_GROUP
S5_STATE = 64
HGRN_WIDTH = D_MODEL // 2
HGRN_HEAD_DIM = 128
HGRN_HEADS = HGRN_WIDTH // HGRN_HEAD_DIM
HGRN_CHUNK = 64
PEER_HEADS = 8
PEER_TOPK = 16
PEER_NKEYS = 128
PEER_NEXPERTS = PEER_NKEYS * PEER_NKEYS
PEER_QDIM = 256
PEER_TOKEN_BLOCK = 128
IN_COLS = S5_WIDTH + 4 * HGRN_WIDTH + 2 * D_MODEL
DEEPNORM_ALPHA = (2.0 * DEPTH) ** 0.25
DEEPNORM_BETA = (8.0 * DEPTH) ** -0.25
LN_EPS = 1e-5
RMS_EPS = 1e-6

kernel_name = "hybrid_s5_hgrn2_peer_deepnorm"


def layer_norm(x, g, b):
    xf = x.astype(jnp.float32)
    mu = jnp.mean(xf, axis=-1, keepdims=True)
    var = jnp.mean(jnp.square(xf - mu), axis=-1, keepdims=True)
    y = (xf - mu) * lax.rsqrt(var + LN_EPS) * g.astype(jnp.float32) + b.astype(jnp.float32)
    return y.astype(x.dtype)


def s5_branch(u, lam_re, lam_im, b_re, b_im, c_re, c_im, d_skip, log_step, w_glu, b_glu):
    bsz, seq, _ = u.shape
    uf = u.astype(jnp.float32).reshape(bsz, seq, S5_GROUPS, S5_GROUP)
    lam = lax.complex(lam_re.astype(jnp.float32), lam_im.astype(jnp.float32))
    dt = jnp.exp(log_step.astype(jnp.float32))[:, None]
    lam_bar = jnp.exp(lam * dt)
    b_mat = lax.complex(b_re.astype(jnp.float32), b_im.astype(jnp.float32))
    b_bar = ((lam_bar - 1.0) / lam)[..., None] * b_mat
    bu = jnp.einsum('bsgh,gph->sbgp', uf.astype(jnp.complex64), b_bar)
    a = jnp.broadcast_to(lam_bar, (seq, 1, S5_GROUPS, S5_STATE))

    def combine(left, right):
        a_l, b_l = left
        a_r, b_r = right
        return a_r * a_l, a_r * b_l + b_r

    _, states = lax.associative_scan(combine, (a, bu), axis=0)
    c_mat = lax.complex(c_re.astype(jnp.float32), c_im.astype(jnp.float32))
    y = jnp.real(jnp.einsum('sbgp,ghp->bsgh', states, c_mat)) + d_skip.astype(jnp.float32) * uf
    y = jax.nn.gelu(y.reshape(bsz, seq, S5_WIDTH))
    y = y * jax.nn.sigmoid(y @ w_glu.astype(jnp.float32) + b_glu.astype(jnp.float32))
    return y.astype(u.dtype)


def hgrn2_branch(q, f_pre, i_in, g, lb, gnorm_w):
    bsz, seq, _ = q.shape
    n_chunks = seq // HGRN_CHUNK
    lbf = lb.astype(jnp.float32)
    f = lbf + (1.0 - lbf) * jax.nn.sigmoid(f_pre.astype(jnp.float32))
    log_f = jnp.log(f)
    k = 1.0 - f
    qf = jax.nn.silu(q.astype(jnp.float32))
    v = i_in.astype(jnp.float32)

    def to_chunks(t):
        return t.reshape(bsz, n_chunks, HGRN_CHUNK, HGRN_HEADS, HGRN_HEAD_DIM).transpose(1, 0, 3, 2, 4)

    causal = jnp.tril(jnp.ones((HGRN_CHUNK, HGRN_CHUNK), dtype=bool))[:, :, None]

    def step(state, inp):
        qc, kc, vc, lc = inp
        bcum = jnp.cumsum(lc, axis=-2)
        inter = jnp.einsum('bhck,bhkv->bhcv', qc * jnp.exp(bcum), state)
        diff = bcum[:, :, :, None, :] - bcum[:, :, None, :, :]
        decay = jnp.exp(jnp.where(causal, diff, -jnp.inf))
        scores = jnp.einsum('bhtk,bhsk,bhtsk->bhts', qc, kc, decay)
        intra = jnp.einsum('bhts,bhsv->bhtv', scores, vc)
        blast = bcum[:, :, -1:, :]
        new_state = (jnp.exp(blast[:, :, 0, :])[..., None] * state
                     + jnp.einsum('bhsk,bhsv->bhkv', kc * jnp.exp(blast - bcum), vc))
        return new_state, inter + intra

    s0 = jnp.zeros((bsz, HGRN_HEADS, HGRN_HEAD_DIM, HGRN_HEAD_DIM), jnp.float32)
    _, o = lax.scan(step, s0, (to_chunks(qf), to_chunks(k), to_chunks(v), to_chunks(log_f)))
    o = o.transpose(1, 0, 3, 2, 4).reshape(bsz, seq, HGRN_HEADS, HGRN_HEAD_DIM)
    o = o * lax.rsqrt(jnp.mean(jnp.square(o), axis=-1, keepdims=True) + RMS_EPS)
    o = o * gnorm_w.astype(jnp.float32).reshape(HGRN_HEADS, HGRN_HEAD_DIM)
    o = o.reshape(bsz, seq, HGRN_WIDTH) * jax.nn.silu(g.astype(jnp.float32))
    return o.astype(q.dtype)


def peer_layer(x, w_pq, sub_keys, peer_u, peer_v):
    bsz, seq, d = x.shape
    xt = x.reshape(-1, PEER_TOKEN_BLOCK, d)
    half = PEER_QDIM // 2

    def block(xb):
        q = (xb @ w_pq).reshape(PEER_TOKEN_BLOCK, PEER_HEADS, 2, half)
        s = jnp.einsum('thpd,hpnd->thpn', q, sub_keys).astype(jnp.float32)
        s1, i1 = lax.top_k(s[:, :, 0], PEER_TOPK)
        s2, i2 = lax.top_k(s[:, :, 1], PEER_TOPK)
        cand = (s1[..., :, None] + s2[..., None, :]).reshape(PEER_TOKEN_BLOCK, PEER_HEADS, PEER_TOPK * PEER_TOPK)
        top, flat = lax.top_k(cand, PEER_TOPK)
        e = (jnp.take_along_axis(i1, flat // PEER_TOPK, axis=-1) * PEER_NKEYS
             + jnp.take_along_axis(i2, flat % PEER_TOPK, axis=-1))
        gates = jax.nn.softmax(top, axis=-1)
        u = peer_u[e]
        act = jax.nn.gelu(jnp.einsum('thkd,td->thk', u, xb).astype(jnp.float32))
        return jnp.einsum('thk,thkd->td', (gates * act).astype(xb.dtype), peer_v[e])

    y = lax.map(block, xt)
    return y.reshape(bsz, seq, d)


def setup_inputs(seed: int = 0) -> dict:
    key = jax.random.key(seed)
    ks = jax.random.split(key, 32)
    nrm = lambda k, shape, scale: jax.random.normal(k, shape, jnp.float32) * scale
    L = DEPTH
    G, P, H, W = S5_GROUPS, S5_STATE, S5_GROUP, S5_WIDTH
    x = nrm(ks[0], (BATCH, SEQ, D_MODEL), 1.0)
    w_in = nrm(ks[1], (L, D_MODEL, IN_COLS), D_MODEL ** -0.5)
    s5_lam_re = -0.5 * jnp.exp(nrm(ks[2], (L, G, P), 0.01))
    s5_lam_im = math.pi * jnp.arange(P, dtype=jnp.float32)[None, None, :] + nrm(ks[3], (L, G, P), 0.01)
    s5_b_re = nrm(ks[4], (L, G, P, H), (2.0 * H) ** -0.5)
    s5_b_im = nrm(ks[5], (L, G, P, H), (2.0 * H) ** -0.5)
    s5_c_re = nrm(ks[6], (L, G, H, P), (2.0 * P) ** -0.5)
    s5_c_im = nrm(ks[7], (L, G, H, P), (2.0 * P) ** -0.5)
    s5_d = nrm(ks[8], (L, G, H), 1.0)
    s5_log_step = jax.random.uniform(ks[9], (L, G), jnp.float32, math.log(1e-3), math.log(1e-1))
    s5_w_glu = nrm(ks[10], (L, W, W), W ** -0.5)
    s5_b_glu = nrm(ks[11], (L, W), 0.01)
    hgrn_lb_logits = 1.0 + nrm(ks[12], (L + 1, HGRN_WIDTH), 0.1)
    hgrn_gnorm_w = 1.0 + nrm(ks[13], (L, HGRN_WIDTH), 0.01)
    w_pa = nrm(ks[14], (L, S5_WIDTH, D_MODEL), S5_WIDTH ** -0.5)
    w_pb = nrm(ks[15], (L, HGRN_WIDTH, D_MODEL), HGRN_WIDTH ** -0.5)
    w_o = nrm(ks[16], (L, D_MODEL, D_MODEL), D_MODEL ** -0.5 * DEEPNORM_BETA)
    ln1_g = 1.0 + nrm(ks[17], (L, D_MODEL), 0.01)
    ln1_b = nrm(ks[18], (L, D_MODEL), 0.01)
    peer_w_q = nrm(ks[19], (L, D_MODEL, PEER_HEADS * PEER_QDIM), D_MODEL ** -0.5)
    peer_sub_keys = nrm(ks[20], (L, PEER_HEADS, 2, PEER_NKEYS, PEER_QDIM // 2), (PEER_QDIM // 2) ** -0.5)
    peer_u = nrm(ks[21], (L, PEER_NEXPERTS, D_MODEL), D_MODEL ** -0.5)
    peer_v = nrm(ks[22], (L, PEER_NEXPERTS, D_MODEL), DEEPNORM_BETA)
    ln2_g = 1.0 + nrm(ks[23], (L, D_MODEL), 0.01)
    ln2_b = nrm(ks[24], (L, D_MODEL), 0.01)
    return {"x": x, "w_in": w_in, "s5_lam_re": s5_lam_re, "s5_lam_im": s5_lam_im,
            "s5_b_re": s5_b_re, "s5_b_im": s5_b_im, "s5_c_re": s5_c_re, "s5_c_im": s5_c_im,
            "s5_d": s5_d, "s5_log_step": s5_log_step, "s5_w_glu": s5_w_glu, "s5_b_glu": s5_b_glu,
            "hgrn_lb_logits": hgrn_lb_logits, "hgrn_gnorm_w": hgrn_gnorm_w,
            "w_pa": w_pa, "w_pb": w_pb, "w_o": w_o, "ln1_g": ln1_g, "ln1_b": ln1_b,
            "peer_w_q": peer_w_q, "peer_sub_keys": peer_sub_keys, "peer_u": peer_u, "peer_v": peer_v,
            "ln2_g": ln2_g, "ln2_b": ln2_b}


def reference(x, w_in, s5_lam_re, s5_lam_im, s5_b_re, s5_b_im, s5_c_re, s5_c_im,
              s5_d, s5_log_step, s5_w_glu, s5_b_glu, hgrn_lb_logits, hgrn_gnorm_w,
              w_pa, w_pb, w_o, ln1_g, ln1_b, peer_w_q, peer_sub_keys, peer_u, peer_v,
              ln2_g, ln2_b):
    lb_all = jnp.cumsum(jax.nn.softmax(hgrn_lb_logits.astype(jnp.float32), axis=0), axis=0)[:DEPTH]
    sizes = [S5_WIDTH, HGRN_WIDTH, HGRN_WIDTH, HGRN_WIDTH, HGRN_WIDTH, D_MODEL, D_MODEL]
    split_at = np.cumsum(sizes)[:-1].tolist()
    for l in range(DEPTH):
        proj = x @ w_in[l]
        u_s5, q_h, f_h, i_h, g_h, gate_a, gate_b = jnp.split(proj, split_at, axis=-1)
        y_a = s5_branch(u_s5, s5_lam_re[l], s5_lam_im[l], s5_b_re[l], s5_b_im[l],
                        s5_c_re[l], s5_c_im[l], s5_d[l], s5_log_step[l], s5_w_glu[l], s5_b_glu[l])
        y_b = hgrn2_branch(q_h, f_h, i_h, g_h, lb_all[l], hgrn_gnorm_w[l])
        merged = jax.nn.sigmoid(gate_a) * (y_a @ w_pa[l]) + jax.nn.sigmoid(gate_b) * (y_b @ w_pb[l])
        mix = merged @ w_o[l]
        x = layer_norm(DEEPNORM_ALPHA * x + mix, ln1_g[l], ln1_b[l])
        ffn = peer_layer(x, peer_w_q[l], peer_sub_keys[l], peer_u[l], peer_v[l])
        x = layer_norm(DEEPNORM_ALPHA * x + ffn, ln2_g[l], ln2_b[l])
    return x
```

```python
import functools
import math

import numpy as np
import jax
import jax.numpy as jnp
from jax import lax
from jax.experimental import pallas as pl
from jax.experimental.pallas import tpu as pltpu

F32 = jnp.float32
BF16 = jnp.bfloat16

V7X_VMEM_LIMIT_BYTES = 56 * 1024 * 1024
LANES = 128
SUBLANES = 8

S5_GROUP = 16
S5_STATE = 64
S5_GROUPS_PER_BLOCK = 8
HGRN_HEAD_DIM = 128
HGRN_CHUNK = 128
PEER_HEADS = 8
PEER_TOPK = 16
PEER_NKEYS = 128
LN_EPS = 1e-5
RMS_EPS = 1e-6


def _cparams(semantics):
    return pltpu.CompilerParams(dimension_semantics=semantics, vmem_limit_bytes=V7X_VMEM_LIMIT_BYTES)


def _dot(a, b):
    return jnp.dot(a, b, preferred_element_type=F32)


def _dot_nt(a, b):
    return lax.dot_general(a, b, (((1,), (1,)), ((), ())), preferred_element_type=F32)


def _dot_tn(a, b):
    return lax.dot_general(a, b, (((0,), (0,)), ((), ())), preferred_element_type=F32)


def _sigmoid(x):
    return 1.0 / (1.0 + jnp.exp(-x))


def _gelu_tanh(x):
    c = math.sqrt(2.0 / math.pi)
    return 0.5 * x * (1.0 + jnp.tanh(c * (x + 0.044715 * (x * x * x))))


def _layer_norm_rows(z, g, b):
    mu = jnp.mean(z, axis=-1, keepdims=True)
    zc = z - mu
    var = jnp.mean(zc * zc, axis=-1, keepdims=True)
    return zc * lax.rsqrt(var + LN_EPS) * g + b


def _proj_kernel(x_ref, w_ref, u_ref, hg_ref, gate_ref, *, s5_w, hg_w, col_chunk):
    xb = x_ref[...].astype(BF16)
    n_cols = w_ref.shape[1]
    for c0 in range(0, n_cols, col_chunk):
        p = _dot(xb, w_ref[:, c0:c0 + col_chunk]).astype(BF16)
        if c0 < s5_w:
            u_ref[:, c0:c0 + col_chunk] = p
        elif c0 < s5_w + hg_w:
            hg_ref[:, c0 - s5_w:c0 - s5_w + col_chunk] = p
        else:
            o = c0 - s5_w - hg_w
            gate_ref[:, o:o + col_chunk] = p


def _proj(x2, w_in_b, bsz, seq, s5_w, hg_w, gate_w, tm):
    n, d = x2.shape
    n_s = seq // tm
    kern = functools.partial(_proj_kernel, s5_w=s5_w, hg_w=hg_w, col_chunk=s5_w)
    return pl.pallas_call(
        kern,
        grid=(bsz, n_s),
        in_specs=[
            pl.BlockSpec((tm, d), lambda b, i: (b * n_s + i, 0)),
            pl.BlockSpec(w_in_b.shape, lambda b, i: (0, 0)),
        ],
        out_specs=[
            pl.BlockSpec((tm, s5_w), lambda b, i: (i, b)),
            pl.BlockSpec((tm, hg_w), lambda b, i: (b * n_s + i, 0)),
            pl.BlockSpec((tm, gate_w), lambda b, i: (b * n_s + i, 0)),
        ],
        out_shape=[
            jax.ShapeDtypeStruct((seq, bsz * s5_w), BF16),
            jax.ShapeDtypeStruct((n, hg_w), BF16),
            jax.ShapeDtypeStruct((n, gate_w), BF16),
        ],
        compiler_params=_cparams(("parallel", "parallel")),
        name="proj",
    )(x2, w_in_b)


def _s5_kernel(u_ref, bblk_ref, cblk_ref, lamr_ref, lami_ref, d_ref, wglu_ref, bglu_ref,
               y_ref, xs_ref, st_ref, *, bsz, tb, n_blk, half):
    @pl.when(pl.program_id(0) == 0)
    def _():
        st_ref[...] = jnp.zeros_like(st_ref)

    in_w = bblk_ref.shape[1]
    ys = []
    for j in range(n_blk):
        uj = u_ref[:, j * in_w:(j + 1) * in_w]
        xs_ref[...] = _dot(uj, bblk_ref[j])
        lr = jnp.broadcast_to(lamr_ref[j], (bsz, half))
        li = jnp.broadcast_to(lami_ref[j], (bsz, half))

        def body(t, carry, lr=lr, li=li):
            sr, si = carry
            rows = pl.ds(pl.multiple_of(t * bsz, bsz), bsz)
            nr = lr * sr - li * si + xs_ref[rows, :half]
            ni = lr * si + li * sr + xs_ref[rows, half:]
            xs_ref[rows, :half] = nr
            xs_ref[rows, half:] = ni
            return nr, ni

        sr, si = lax.fori_loop(0, tb, body, (st_ref[j, :, :half], st_ref[j, :, half:]), unroll=4)
        st_ref[j, :, :half] = sr
        st_ref[j, :, half:] = si
        ys.append(_dot(xs_ref[...].astype(BF16), cblk_ref[j]))
    y = jnp.concatenate(ys, axis=1) + d_ref[...] * u_ref[...].astype(F32)
    y = _gelu_tanh(y)
    z = _dot(y.astype(BF16), wglu_ref[...]) + bglu_ref[...]
    y_ref[...] = (y * _sigmoid(z)).astype(BF16)


def _s5_params(lam_re, lam_im, b_re, b_im, c_re, c_im, d_skip, log_step):
    g, p = lam_re.shape
    h = b_re.shape[-1]
    gb = S5_GROUPS_PER_BLOCK
    nb = g // gb
    lam = lax.complex(lam_re.astype(F32), lam_im.astype(F32))
    dt = jnp.exp(log_step.astype(F32))[:, None]
    lam_bar = jnp.exp(lam * dt)
    b_bar = ((lam_bar - 1.0) / lam)[..., None] * lax.complex(b_re.astype(F32), b_im.astype(F32))
    eye = jnp.eye(gb, dtype=F32)

    def in_blk(m):
        m = m.reshape(nb, gb, p, h)
        return jnp.einsum('jgph,gk->jghkp', m, eye).reshape(nb, gb * h, gb * p)

    def out_blk(m):
        m = m.reshape(nb, gb, h, p)
        return jnp.einsum('jghp,gk->jgpkh', m, eye).reshape(nb, gb * p, gb * h)

    bblk = jnp.concatenate([in_blk(jnp.real(b_bar)), in_blk(jnp.imag(b_bar))], axis=2).astype(BF16)
    cblk = jnp.concatenate([out_blk(c_re.astype(F32)), out_blk(-c_im.astype(F32))], axis=1).astype(BF16)
    lamr = jnp.real(lam_bar).reshape(nb, 1, gb * p)
    lami = jnp.imag(lam_bar).reshape(nb, 1, gb * p)
    return bblk, cblk, lamr, lami, d_skip.astype(F32).reshape(1, g * h)


def _s5(u_tm, params, w_glu_b, b_glu, bsz, seq, tb):
    bblk, cblk, lamr, lami, d_row = params
    n, w = u_tm.shape
    n_blk, in_w, two_half = bblk.shape
    half = two_half // 2
    rows = tb * bsz
    kern = functools.partial(_s5_kernel, bsz=bsz, tb=tb, n_blk=n_blk, half=half)
    const = lambda a: pl.BlockSpec(a.shape, lambda i: (0,) * a.ndim)
    return pl.pallas_call(
        kern,
        grid=(seq // tb,),
        in_specs=[pl.BlockSpec((rows, w), lambda i: (i, 0)),
                  const(bblk), const(cblk), const(lamr), const(lami), const(d_row),
                  const(w_glu_b), const(b_glu)],
        out_specs=pl.BlockSpec((rows, w), lambda i: (i, 0)),
        out_shape=jax.ShapeDtypeStruct((n, w), BF16),
        scratch_shapes=[pltpu.VMEM((rows, two_half), F32), pltpu.VMEM((n_blk, bsz, two_half), F32)],
        compiler_params=_cparams(("arbitrary",)),
        name="s5",
    )(u_tm, bblk, cblk, lamr, lami, d_row, w_glu_b, b_glu)


def _hgrn_tables(c):
    n_lev = int(math.log2(c))
    t = np.arange(c)[:, None]
    r = np.arange(c)[None, :]
    sel = []
    masks = []
    for lev in range(n_lev):
        w = c >> (lev + 1)
        pos = t % (2 * w)
        a = t - pos + w - 1
        upper = pos >= w
        sel.append(np.where(upper, (r > a) & (r <= t), (r > t) & (r <= a)))
        tt, ss = np.arange(c)[:, None], np.arange(c)[None, :]
        masks.append((tt // (2 * w) == ss // (2 * w)) & (tt % (2 * w) >= w) & (ss % (2 * w) < w))
    sel.append(r <= t)
    sel.append(r > t)
    masks.append(np.eye(c, dtype=bool))
    sel = np.concatenate(sel, axis=0).astype(np.float32)
    masks = np.stack(masks, axis=0).astype(np.float32)
    return jnp.asarray(sel, BF16), jnp.asarray(masks, F32)


def _hgrn_kernel(hg_ref, sel_ref, mask_ref, lb_ref, gw_ref, y_ref, st_ref, *, c, n_chunks, width, heads):
    @pl.when(pl.program_id(1) == 0)
    def _():
        st_ref[...] = jnp.zeros_like(st_ref)

    n_lev = mask_ref.shape[0] - 1
    dh = width // heads
    lb = lb_ref[...]
    gw = gw_ref[...]

    def chunk(ci, carry):
        rows = pl.ds(pl.multiple_of(ci * c, c), c)
        q = hg_ref[0, rows, 0:width].astype(F32)
        fp = hg_ref[0, rows, width:2 * width].astype(F32)
        v = hg_ref[0, rows, 2 * width:3 * width]
        g = hg_ref[0, rows, 3 * width:4 * width].astype(F32)
        f = lb + (1.0 - lb) * _sigmoid(fp)
        lf = jnp.log(f)
        k = 1.0 - f
        qf = q * _sigmoid(q)
        h1 = lf.astype(BF16)
        r1 = lf - h1.astype(F32)
        h2 = r1.astype(BF16)
        h3 = (r1 - h2.astype(F32)).astype(BF16)
        sel = sel_ref[...]
        e_all = jnp.exp(_dot(sel, h1) + _dot(sel, h2) + _dot(sel, h3))
        e_cum = e_all[n_lev * c:(n_lev + 1) * c]
        e_suf = e_all[(n_lev + 1) * c:(n_lev + 2) * c]
        outs = []
        for hd in range(heads):
            cols = slice(hd * dh, (hd + 1) * dh)
            qh, kh = qf[:, cols], k[:, cols]
            vh = v[:, cols]
            sc = mask_ref[n_lev] * _dot_nt(qh.astype(BF16), kh.astype(BF16))
            for lev in range(n_lev):
                el = e_all[lev * c:(lev + 1) * c, cols]
                sc = sc + mask_ref[lev] * _dot_nt((qh * el).astype(BF16), (kh * el).astype(BF16))
            st = st_ref[hd]
            o = _dot_nt((qh * e_cum[:, cols]).astype(BF16), st.astype(BF16)) + _dot(sc.astype(BF16), vh)
            e_tot = e_cum[c - 1:c, cols]
            st_ref[hd] = st * e_tot + _dot_tn(vh, (kh * e_suf[:, cols]).astype(BF16))
            o = o * lax.rsqrt(jnp.mean(o * o, axis=-1, keepdims=True) + RMS_EPS)
            outs.append(o)
        o = jnp.concatenate(outs, axis=1) * gw * (g * _sigmoid(g))
        y_ref[0, rows, :] = o.astype(BF16)
        return carry

    lax.fori_loop(0, n_chunks, chunk, 0)


def _hgrn(hg3, lb_row, gw_row, heads, tbh):
    bsz, seq, w4 = hg3.shape
    width = w4 // 4
    c = HGRN_CHUNK
    sel, masks = _hgrn_tables(c)
    kern = functools.partial(_hgrn_kernel, c=c, n_chunks=tbh // c, width=width, heads=heads)
    const = lambda a: pl.BlockSpec(a.shape, lambda b, i: (0,) * a.ndim)
    return pl.pallas_call(
        kern,
        grid=(bsz, seq // tbh),
        in_specs=[pl.BlockSpec((1, tbh, w4), lambda b, i: (b, i, 0)),
                  const(sel), const(masks), const(lb_row), const(gw_row)],
        out_specs=pl.BlockSpec((1, tbh, width), lambda b, i: (b, i, 0)),
        out_shape=jax.ShapeDtypeStruct((bsz, seq, width), BF16),
        scratch_shapes=[pltpu.VMEM((heads, width // heads, width // heads), F32)],
        compiler_params=_cparams(("parallel", "arbitrary")),
        name="hgrn",
    )(hg3, sel, masks, lb_row, gw_row)


def _merge_kernel(x_ref, ya_ref, yb_ref, gate_ref, wpa_ref, wpb_ref, wo_ref, g_ref, b_ref,
                  x1_ref, x1b_ref, *, alpha, d):
    ga = gate_ref[:, :d].astype(F32)
    gb = gate_ref[:, d:].astype(F32)
    merged = _sigmoid(ga) * _dot(ya_ref[...], wpa_ref[...]) + _sigmoid(gb) * _dot(yb_ref[...], wpb_ref[...])
    mix = _dot(merged.astype(BF16), wo_ref[...])
    x1 = _layer_norm_rows(alpha * x_ref[...] + mix, g_ref[...], b_ref[...])
    x1_ref[...] = x1
    x1b_ref[...] = x1.astype(BF16)


def _merge(x2, ya_tm, yb2, gates, wpa_b, wpb_b, wo_b, g_row, b_row, bsz, seq, alpha, tm):
    n, d = x2.shape
    w = yb2.shape[1]
    n_s = seq // tm
    kern = functools.partial(_merge_kernel, alpha=alpha, d=d)
    const = lambda a: pl.BlockSpec(a.shape, lambda b, i: (0,) * a.ndim)
    row = lambda cols: pl.BlockSpec((tm, cols), lambda b, i: (b * n_s + i, 0))
    return pl.pallas_call(
        kern,
        grid=(bsz, n_s),
        in_specs=[row(d),
                  pl.BlockSpec((tm, w), lambda b, i: (i, b)),
                  row(w), row(2 * d),
                  const(wpa_b), const(wpb_b), const(wo_b), const(g_row), const(b_row)],
        out_specs=[row(d), row(d)],
        out_shape=[jax.ShapeDtypeStruct((n, d), F32), jax.ShapeDtypeStruct((n, d), BF16)],
        compiler_params=_cparams(("parallel", "parallel")),
        name="merge",
    )(x2, ya_tm, yb2, gates, wpa_b, wpb_b, wo_b, g_row, b_row)


def _stage1(x, w_in, s5_lam_re, s5_lam_im, s5_b_re, s5_b_im, s5_c_re, s5_c_im, s5_d, s5_log_step,
            s5_w_glu, s5_b_glu, lb, gnorm_w, w_pa, w_pb, w_o, ln1_g, ln1_b, alpha):
    bsz, seq, d = x.shape
    n = bsz * seq
    s5_w = s5_w_glu.shape[0]
    hg_w = 4 * gnorm_w.shape[0]
    gate_w = 2 * d
    tm = min(512, seq)
    x2 = x.reshape(n, d)
    u_tm, hg, gates = _proj(x2, w_in.astype(BF16), bsz, seq, s5_w, hg_w, gate_w, tm)
    s5p = _s5_params(s5_lam_re, s5_lam_im, s5_b_re, s5_b_im, s5_c_re, s5_c_im, s5_d, s5_log_step)
    ya = _s5(u_tm.reshape(seq * bsz, s5_w), s5p, s5_w_glu.astype(BF16),
             s5_b_glu.astype(F32).reshape(1, s5_w), bsz, seq, tb=min(128, seq))
    yb = _hgrn(hg.reshape(bsz, seq, hg_w), lb.reshape(1, -1), gnorm_w.astype(F32).reshape(1, -1),
               heads=gnorm_w.shape[0] // HGRN_HEAD_DIM, tbh=min(512, seq))
    return _merge(x2, ya.reshape(seq, bsz * s5_w), yb.reshape(n, -1), gates,
                  w_pa.astype(BF16), w_pb.astype(BF16), w_o.astype(BF16),
                  ln1_g.astype(F32).reshape(1, d), ln1_b.astype(F32).reshape(1, d), bsz, seq, alpha, tm)


def _top_sorted(s, k):
    rows = []
    cur = s
    for j in range(k):
        mx = jnp.max(cur, axis=0, keepdims=True)
        rows.append(mx)
        if j + 1 < k:
            cur = jnp.where(cur == mx, -jnp.inf, cur)
    return rows


def _route_kernel(xb_ref, wq_ref, keys_ref, thr_ref, p1_ref, s2_ref, p2_ref, s_scr, *, heads, topk):
    nk = keys_ref.shape[1]
    q_t = _dot_nt(wq_ref[...], xb_ref[...])
    for hp in range(2 * heads):
        s_scr[hp] = _dot(keys_ref[hp], q_t[hp * nk:(hp + 1) * nk, :].astype(BF16))

    def head(h, carry):
        s1 = s_scr[2 * h]
        s2 = s_scr[2 * h + 1]
        a = _top_sorted(s1, topk)
        b = _top_sorted(s2, topk)
        b_all = jnp.concatenate(b, axis=0)
        half_rows = topk // 2
        l_idx = lax.broadcasted_iota(jnp.int32, (half_rows, 1), 0)
        cands = [a[0] + b_all]
        for j in range(1, topk):
            cj = a[j] + b_all[:half_rows]
            cands.append(jnp.where(l_idx < topk // (j + 1), cj, -jnp.inf))
        cur = jnp.concatenate(cands, axis=0)
        m = a[0] + b[0]
        z = jnp.zeros_like(m)
        tau = m
        for r in range(topk):
            tau = jnp.max(cur, axis=0, keepdims=True)
            z = z + jnp.exp(tau - m)
            if r + 1 < topk:
                cur = jnp.where(cur == tau, -jnp.inf, cur)
        thr = jnp.full(s1.shape, jnp.inf, F32)
        for j in range(topk):
            cj = cands[j]
            bj = b_all if j == 0 else b_all[:half_rows]
            tj = jnp.min(jnp.where(cj >= tau, bj, jnp.inf), axis=0, keepdims=True)
            thr = jnp.minimum(thr, jnp.where(s1 == a[j], tj, jnp.inf))
        thr_ref[h] = thr
        p1_ref[h] = jnp.exp(s1 - a[0])
        s2_ref[h] = s2
        p2_ref[h] = jnp.exp(s2 - b[0]) / z
        return carry

    lax.fori_loop(0, heads, head, 0)


def _route(x1b, wq_t, keys, tt):
    n, d = x1b.shape
    hp, nk, _ = keys.shape
    heads = hp // 2
    kern = functools.partial(_route_kernel, heads=heads, topk=PEER_TOPK)
    out = jax.ShapeDtypeStruct((heads, nk, n), F32)
    ospec = pl.BlockSpec((heads, nk, tt), lambda i: (0, 0, i))
    return pl.pallas_call(
        kern,
        grid=(n // tt,),
        in_specs=[pl.BlockSpec((tt, d), lambda i: (i, 0)),
                  pl.BlockSpec(wq_t.shape, lambda i: (0, 0)),
                  pl.BlockSpec(keys.shape, lambda i: (0, 0, 0))],
        out_specs=[ospec, ospec, ospec, ospec],
        out_shape=[out, out, out, out],
        scratch_shapes=[pltpu.VMEM((hp, nk, tt), F32)],
        compiler_params=_cparams(("parallel",)),
        name="route",
    )(x1b, wq_t, keys)


def _peer_kernel(xb_ref, x_ref, u_ref, vt_ref, thr_ref, p1_ref, s2_ref, p2_ref, g_ref, b_ref,
                 out_ref, acc_ref, h_ref, a_ref, *, heads, nk, alpha):
    j = pl.program_id(1)
    tt = xb_ref.shape[0]
    n_i1 = u_ref.shape[0] // nk

    @pl.when(j == 0)
    def _():
        acc_ref[...] = jnp.zeros_like(acc_ref)

    h_ref[...] = _dot_nt(u_ref[...], xb_ref[...])

    i1_rows = pl.ds(pl.multiple_of(j * n_i1, n_i1), n_i1)
    for lt in range(tt // LANES):
        cols = slice(lt * LANES, (lt + 1) * LANES)
        thr_t = [thr_ref[h, i1_rows, cols] for h in range(heads)]
        p1_t = [p1_ref[h, i1_rows, cols] for h in range(heads)]
        for ii in range(n_i1):
            rows = slice(ii * nk, (ii + 1) * nk)
            w = jnp.zeros((nk, LANES), F32)
            for h in range(heads):
                sel = jnp.where(s2_ref[h, :, cols] >= thr_t[h][ii:ii + 1], p2_ref[h, :, cols], 0.0)
                w = w + sel * p1_t[h][ii:ii + 1]
            a_ref[rows, cols] = (w * _gelu_tanh(h_ref[rows, cols])).astype(BF16)
    acc_ref[...] += _dot(vt_ref[...], a_ref[...])

    @pl.when(j == pl.num_programs(1) - 1)
    def _():
        z = alpha * x_ref[...] + acc_ref[...].T
        out_ref[...] = _layer_norm_rows(z, g_ref[...], b_ref[...])


def _peer(x1, x1b, u_b, vt_b, routing, g_row, b_row, alpha, tt, chunk):
    n, d = x1.shape
    thr, p1, s2, p2 = routing
    heads, nk, _ = thr.shape
    n_exp = u_b.shape[0]
    assert chunk == SUBLANES * nk, "one grid step covers one sublane tile of i1 values"
    kern = functools.partial(_peer_kernel, heads=heads, nk=nk, alpha=alpha)
    rspec = pl.BlockSpec((heads, nk, tt), lambda i, j: (0, 0, i))
    return pl.pallas_call(
        kern,
        grid=(n // tt, n_exp // chunk),
        in_specs=[pl.BlockSpec((tt, d), lambda i, j: (i, 0)),
                  pl.BlockSpec((tt, d), lambda i, j: (i, 0)),
                  pl.BlockSpec((chunk, d), lambda i, j: (j, 0)),
                  pl.BlockSpec((d, chunk), lambda i, j: (0, j)),
                  rspec, rspec, rspec, rspec,
                  pl.BlockSpec((1, d), lambda i, j: (0, 0)),
                  pl.BlockSpec((1, d), lambda i, j: (0, 0))],
        out_specs=pl.BlockSpec((tt, d), lambda i, j: (i, 0)),
        out_shape=jax.ShapeDtypeStruct((n, d), F32),
        scratch_shapes=[pltpu.VMEM((d, tt), F32), pltpu.VMEM((chunk, tt), F32), pltpu.VMEM((chunk, tt), BF16)],
        compiler_params=_cparams(("parallel", "arbitrary")),
        name="peer",
    )(x1b, x1, u_b, vt_b, thr, p1, s2, p2, g_row, b_row)


def _stage2(x1, x1b, w_pq, sub_keys, peer_u, peer_v, ln2_g, ln2_b, alpha):
    n, d = x1.shape
    heads, _, nk, half = sub_keys.shape
    tt = min(512, n)
    routing = _route(x1b, w_pq.T.astype(BF16), sub_keys.reshape(heads * 2, nk, half).astype(BF16), tt)
    return _peer(x1, x1b, peer_u.astype(BF16), peer_v.T.astype(BF16), routing,
                 ln2_g.astype(F32).reshape(1, d), ln2_b.astype(F32).reshape(1, d), alpha, tt, chunk=SUBLANES * nk)


def kernel(x, w_in, s5_lam_re, s5_lam_im, s5_b_re, s5_b_im, s5_c_re, s5_c_im, s5_d, s5_log_step, s5_w_glu,
           s5_b_glu, hgrn_lb_logits, hgrn_gnorm_w, w_pa, w_pb, w_o, ln1_g, ln1_b, peer_w_q, peer_sub_keys,
           peer_u, peer_v, ln2_g, ln2_b):
    depth = w_in.shape[0]
    alpha = (2.0 * depth) ** 0.25
    lb_all = jnp.cumsum(jax.nn.softmax(hgrn_lb_logits.astype(F32), axis=0), axis=0)[:depth]
    bsz, seq, d = x.shape
    for l in range(depth):
        x1, x1b = _stage1(x, w_in[l], s5_lam_re[l], s5_lam_im[l], s5_b_re[l], s5_b_im[l], s5_c_re[l], s5_c_im[l],
                          s5_d[l], s5_log_step[l], s5_w_glu[l], s5_b_glu[l], lb_all[l], hgrn_gnorm_w[l],
                          w_pa[l], w_pb[l], w_o[l], ln1_g[l], ln1_b[l], alpha)
        x = _stage2(x1, x1b, peer_w_q[l], peer_sub_keys[l], peer_u[l], peer_v[l], ln2_g[l], ln2_b[l],
                    alpha).reshape(bsz, seq, d)
    return x
```

```python
import functools
import math

import numpy as np
import jax
import jax.numpy as jnp
from jax import lax
from jax.experimental import pallas as pl
from jax.experimental.pallas import tpu as pltpu

F32 = jnp.float32
BF16 = jnp.bfloat16

V7X_VMEM_LIMIT_BYTES = 56 * 1024 * 1024
LANES = 128
SUBLANES = 8
BF16_SUBLANES = 16

S5_GROUP = 16
S5_STATE = 64
S5_GROUPS_PER_BLOCK = 8
HGRN_HEAD_DIM = 128
HGRN_CHUNK = 128
PEER_HEADS = 8
PEER_TOPK = 16
PEER_NKEYS = 128
LN_EPS = 1e-5
RMS_EPS = 1e-6


def _cparams(semantics):
    return pltpu.CompilerParams(dimension_semantics=semantics, vmem_limit_bytes=V7X_VMEM_LIMIT_BYTES)


def _dot(a, b):
    return jnp.dot(a, b, preferred_element_type=F32)


def _dot_nt(a, b):
    return lax.dot_general(a, b, (((1,), (1,)), ((), ())), preferred_element_type=F32)


def _dot_tn(a, b):
    return lax.dot_general(a, b, (((0,), (0,)), ((), ())), preferred_element_type=F32)


def _sigmoid(x):
    return 1.0 / (1.0 + jnp.exp(-x))


def _gelu_tanh(x):
    c = math.sqrt(2.0 / math.pi)
    return 0.5 * x * (1.0 + jnp.tanh(c * (x + 0.044715 * (x * x * x))))


def _layer_norm_rows(z, g, b):
    mu = jnp.mean(z, axis=-1, keepdims=True)
    zc = z - mu
    var = jnp.mean(zc * zc, axis=-1, keepdims=True)
    return zc * lax.rsqrt(var + LN_EPS) * g + b


def _proj_kernel(x_ref, w_ref, u_ref, hg_ref, gate_ref, *, s5_w, hg_w, col_chunk):
    xb = x_ref[...].astype(BF16)
    n_cols = w_ref.shape[1]
    for c0 in range(0, n_cols, col_chunk):
        p = _dot(xb, w_ref[:, c0:c0 + col_chunk]).astype(BF16)
        if c0 < s5_w:
            u_ref[:, c0:c0 + col_chunk] = p
        elif c0 < s5_w + hg_w:
            hg_ref[:, c0 - s5_w:c0 - s5_w + col_chunk] = p
        else:
            o = c0 - s5_w - hg_w
            gate_ref[:, o:o + col_chunk] = p


def _proj(x2, w_in_b, bsz, seq, s5_w, hg_w, gate_w, tm):
    n, d = x2.shape
    n_s = seq // tm
    kern = functools.partial(_proj_kernel, s5_w=s5_w, hg_w=hg_w, col_chunk=s5_w)
    return pl.pallas_call(
        kern,
        grid=(bsz, n_s),
        in_specs=[
            pl.BlockSpec((tm, d), lambda b, i: (b * n_s + i, 0)),
            pl.BlockSpec(w_in_b.shape, lambda b, i: (0, 0)),
        ],
        out_specs=[
            pl.BlockSpec((tm, s5_w), lambda b, i: (i, b)),
            pl.BlockSpec((tm, hg_w), lambda b, i: (b * n_s + i, 0)),
            pl.BlockSpec((tm, gate_w), lambda b, i: (b * n_s + i, 0)),
        ],
        out_shape=[
            jax.ShapeDtypeStruct((seq, bsz * s5_w), BF16),
            jax.ShapeDtypeStruct((n, hg_w), BF16),
            jax.ShapeDtypeStruct((n, gate_w), BF16),
        ],
        compiler_params=_cparams(("parallel", "parallel")),
        name="proj",
    )(x2, w_in_b)


def _s5_kernel(u_ref, bblk_ref, cblk_ref, lamr_ref, lami_ref, d_ref, wglu_ref, bglu_ref,
               y_ref, xs_ref, st_ref, *, bsz, tb, n_blk, half):
    @pl.when(pl.program_id(0) == 0)
    def _():
        st_ref[...] = jnp.zeros_like(st_ref)

    in_w = bblk_ref.shape[1]
    ys = []
    for j in range(n_blk):
        uj = u_ref[:, j * in_w:(j + 1) * in_w]
        xs_ref[...] = _dot(uj, bblk_ref[j])
        lr = jnp.broadcast_to(lamr_ref[j], (bsz, half))
        li = jnp.broadcast_to(lami_ref[j], (bsz, half))

        def body(t, carry, lr=lr, li=li):
            sr, si = carry
            rows = pl.ds(pl.multiple_of(t * bsz, bsz), bsz)
            nr = lr * sr - li * si + xs_ref[rows, :half]
            ni = lr * si + li * sr + xs_ref[rows, half:]
            xs_ref[rows, :half] = nr
            xs_ref[rows, half:] = ni
            return nr, ni

        sr, si = lax.fori_loop(0, tb, body, (st_ref[j, :, :half], st_ref[j, :, half:]), unroll=4)
        st_ref[j, :, :half] = sr
        st_ref[j, :, half:] = si
        ys.append(_dot(xs_ref[...].astype(BF16), cblk_ref[j]))
    y = jnp.concatenate(ys, axis=1) + d_ref[...] * u_ref[...].astype(F32)
    y = _gelu_tanh(y)
    z = _dot(y.astype(BF16), wglu_ref[...]) + bglu_ref[...]
    y_ref[...] = (y * _sigmoid(z)).astype(BF16)


def _s5_params(lam_re, lam_im, b_re, b_im, c_re, c_im, d_skip, log_step):
    g, p = lam_re.shape
    h = b_re.shape[-1]
    gb = S5_GROUPS_PER_BLOCK
    nb = g // gb
    lr, li = lam_re.astype(F32), lam_im.astype(F32)
    dt = jnp.exp(log_step.astype(F32))[:, None]
    mag = jnp.exp(lr * dt)
    lbr, lbi = mag * jnp.cos(li * dt), mag * jnp.sin(li * dt)
    den = lr * lr + li * li
    cr = ((lbr - 1.0) * lr + lbi * li) / den
    ci = (lbi * lr - (lbr - 1.0) * li) / den
    br, bi = b_re.astype(F32), b_im.astype(F32)
    bbar_r = cr[..., None] * br - ci[..., None] * bi
    bbar_i = cr[..., None] * bi + ci[..., None] * br
    eye = jnp.eye(gb, dtype=F32)

    def in_blk(m):
        m = m.reshape(nb, gb, p, h)
        return jnp.einsum('jgph,gk->jghkp', m, eye).reshape(nb, gb * h, gb * p)

    def out_blk(m):
        m = m.reshape(nb, gb, h, p)
        return jnp.einsum('jghp,gk->jgpkh', m, eye).reshape(nb, gb * p, gb * h)

    bblk = jnp.concatenate([in_blk(bbar_r), in_blk(bbar_i)], axis=2).astype(BF16)
    cblk = jnp.concatenate([out_blk(c_re.astype(F32)), out_blk(-c_im.astype(F32))], axis=1).astype(BF16)
    lamr = lbr.reshape(nb, 1, gb * p)
    lami = lbi.reshape(nb, 1, gb * p)
    return bblk, cblk, lamr, lami, d_skip.astype(F32).reshape(1, g * h)


def _s5(u_tm, params, w_glu_b, b_glu, bsz, seq, tb):
    bblk, cblk, lamr, lami, d_row = params
    n, w = u_tm.shape
    n_blk, in_w, two_half = bblk.shape
    half = two_half // 2
    rows = tb * bsz
    kern = functools.partial(_s5_kernel, bsz=bsz, tb=tb, n_blk=n_blk, half=half)
    const = lambda a: pl.BlockSpec(a.shape, lambda i: (0,) * a.ndim)
    return pl.pallas_call(
        kern,
        grid=(seq // tb,),
        in_specs=[pl.BlockSpec((rows, w), lambda i: (i, 0)),
                  const(bblk), const(cblk), const(lamr), const(lami), const(d_row),
                  const(w_glu_b), const(b_glu)],
        out_specs=pl.BlockSpec((rows, w), lambda i: (i, 0)),
        out_shape=jax.ShapeDtypeStruct((n, w), BF16),
        scratch_shapes=[pltpu.VMEM((rows, two_half), F32), pltpu.VMEM((n_blk, bsz, two_half), F32)],
        compiler_params=_cparams(("arbitrary",)),
        name="s5",
    )(u_tm, bblk, cblk, lamr, lami, d_row, w_glu_b, b_glu)


def _hgrn_tables(c):
    n_lev = int(math.log2(c))
    t = np.arange(c)[:, None]
    r = np.arange(c)[None, :]
    sel = []
    masks = []
    for lev in range(n_lev):
        w = c >> (lev + 1)
        pos = t % (2 * w)
        a = t - pos + w - 1
        upper = pos >= w
        sel.append(np.where(upper, (r > a) & (r <= t), (r > t) & (r <= a)))
        tt, ss = np.arange(c)[:, None], np.arange(c)[None, :]
        masks.append((tt // (2 * w) == ss // (2 * w)) & (tt % (2 * w) >= w) & (ss % (2 * w) < w))
    sel.append(r <= t)
    sel.append(r > t)
    masks.append(np.eye(c, dtype=bool))
    sel = np.concatenate(sel, axis=0).astype(np.float32)
    masks = np.stack(masks, axis=0).astype(np.float32)
    return jnp.asarray(sel, BF16), jnp.asarray(masks, F32)


def _hgrn_kernel(hg_ref, sel_ref, mask_ref, lb_ref, gw_ref, y_ref, st_ref, *, c, n_chunks, width, heads):
    @pl.when(pl.program_id(1) == 0)
    def _():
        st_ref[...] = jnp.zeros_like(st_ref)

    n_lev = mask_ref.shape[0] - 1
    dh = width // heads
    lb = lb_ref[...]
    gw = gw_ref[...]

    def chunk(ci, carry):
        rows = pl.ds(pl.multiple_of(ci * c, c), c)
        q = hg_ref[0, rows, 0:width].astype(F32)
        fp = hg_ref[0, rows, width:2 * width].astype(F32)
        v = hg_ref[0, rows, 2 * width:3 * width]
        g = hg_ref[0, rows, 3 * width:4 * width].astype(F32)
        f = lb + (1.0 - lb) * _sigmoid(fp)
        lf = jnp.log(f)
        k = 1.0 - f
        qf = q * _sigmoid(q)
        h1 = lf.astype(BF16)
        r1 = lf - h1.astype(F32)
        h2 = r1.astype(BF16)
        h3 = (r1 - h2.astype(F32)).astype(BF16)
        sel = sel_ref[...]
        e_all = jnp.exp(_dot(sel, h1) + _dot(sel, h2) + _dot(sel, h3))
        e_cum = e_all[n_lev * c:(n_lev + 1) * c]
        e_suf = e_all[(n_lev + 1) * c:(n_lev + 2) * c]
        outs = []
        for hd in range(heads):
            cols = slice(hd * dh, (hd + 1) * dh)
            qh, kh = qf[:, cols], k[:, cols]
            vh = v[:, cols]
            sc = mask_ref[n_lev] * _dot_nt(qh.astype(BF16), kh.astype(BF16))
            for lev in range(n_lev):
                el = e_all[lev * c:(lev + 1) * c, cols]
                sc = sc + mask_ref[lev] * _dot_nt((qh * el).astype(BF16), (kh * el).astype(BF16))
            st = st_ref[hd]
            o = _dot_nt((qh * e_cum[:, cols]).astype(BF16), st.astype(BF16)) + _dot(sc.astype(BF16), vh)
            e_tot = e_cum[c - 1:c, cols]
            st_ref[hd] = st * e_tot + _dot_tn(vh, (kh * e_suf[:, cols]).astype(BF16))
            o = o * lax.rsqrt(jnp.mean(o * o, axis=-1, keepdims=True) + RMS_EPS)
            outs.append(o)
        o = jnp.concatenate(outs, axis=1) * gw * (g * _sigmoid(g))
        y_ref[0, rows, :] = o.astype(BF16)
        return carry

    lax.fori_loop(0, n_chunks, chunk, 0)


def _hgrn(hg3, lb_row, gw_row, heads, tbh):
    bsz, seq, w4 = hg3.shape
    width = w4 // 4
    c = HGRN_CHUNK
    sel, masks = _hgrn_tables(c)
    kern = functools.partial(_hgrn_kernel, c=c, n_chunks=tbh // c, width=width, heads=heads)
    const = lambda a: pl.BlockSpec(a.shape, lambda b, i: (0,) * a.ndim)
    return pl.pallas_call(
        kern,
        grid=(bsz, seq // tbh),
        in_specs=[pl.BlockSpec((1, tbh, w4), lambda b, i: (b, i, 0)),
                  const(sel), const(masks), const(lb_row), const(gw_row)],
        out_specs=pl.BlockSpec((1, tbh, width), lambda b, i: (b, i, 0)),
        out_shape=jax.ShapeDtypeStruct((bsz, seq, width), BF16),
        scratch_shapes=[pltpu.VMEM((heads, width // heads, width // heads), F32)],
        compiler_params=_cparams(("parallel", "arbitrary")),
        name="hgrn",
    )(hg3, sel, masks, lb_row, gw_row)


def _merge_kernel(x_ref, ya_ref, yb_ref, gate_ref, wpa_ref, wpb_ref, wo_ref, g_ref, b_ref,
                  x1_ref, x1b_ref, *, alpha, d):
    ga = gate_ref[:, :d].astype(F32)
    gb = gate_ref[:, d:].astype(F32)
    merged = _sigmoid(ga) * _dot(ya_ref[...], wpa_ref[...]) + _sigmoid(gb) * _dot(yb_ref[...], wpb_ref[...])
    mix = _dot(merged.astype(BF16), wo_ref[...])
    x1 = _layer_norm_rows(alpha * x_ref[...] + mix, g_ref[...], b_ref[...])
    x1_ref[...] = x1
    x1b_ref[...] = x1.astype(BF16)


def _merge(x2, ya_tm, yb2, gates, wpa_b, wpb_b, wo_b, g_row, b_row, bsz, seq, alpha, tm):
    n, d = x2.shape
    w = yb2.shape[1]
    n_s = seq // tm
    kern = functools.partial(_merge_kernel, alpha=alpha, d=d)
    const = lambda a: pl.BlockSpec(a.shape, lambda b, i: (0,) * a.ndim)
    row = lambda cols: pl.BlockSpec((tm, cols), lambda b, i: (b * n_s + i, 0))
    return pl.pallas_call(
        kern,
        grid=(bsz, n_s),
        in_specs=[row(d),
                  pl.BlockSpec((tm, w), lambda b, i: (i, b)),
                  row(w), row(2 * d),
                  const(wpa_b), const(wpb_b), const(wo_b), const(g_row), const(b_row)],
        out_specs=[row(d), row(d)],
        out_shape=[jax.ShapeDtypeStruct((n, d), F32), jax.ShapeDtypeStruct((n, d), BF16)],
        compiler_params=_cparams(("parallel", "parallel")),
        name="merge",
    )(x2, ya_tm, yb2, gates, wpa_b, wpb_b, wo_b, g_row, b_row)


def _stage1(x, w_in, s5_lam_re, s5_lam_im, s5_b_re, s5_b_im, s5_c_re, s5_c_im, s5_d, s5_log_step,
            s5_w_glu, s5_b_glu, lb, gnorm_w, w_pa, w_pb, w_o, ln1_g, ln1_b, alpha):
    bsz, seq, d = x.shape
    n = bsz * seq
    s5_w = s5_w_glu.shape[0]
    hg_w = 4 * gnorm_w.shape[0]
    gate_w = 2 * d
    tm = min(512, seq)
    x2 = x.reshape(n, d)
    u_tm, hg, gates = _proj(x2, w_in.astype(BF16), bsz, seq, s5_w, hg_w, gate_w, tm)
    s5p = _s5_params(s5_lam_re, s5_lam_im, s5_b_re, s5_b_im, s5_c_re, s5_c_im, s5_d, s5_log_step)
    ya = _s5(u_tm.reshape(seq * bsz, s5_w), s5p, s5_w_glu.astype(BF16),
             s5_b_glu.astype(F32).reshape(1, s5_w), bsz, seq, tb=min(128, seq))
    yb = _hgrn(hg.reshape(bsz, seq, hg_w), lb.reshape(1, -1), gnorm_w.astype(F32).reshape(1, -1),
               heads=gnorm_w.shape[0] // HGRN_HEAD_DIM, tbh=min(512, seq))
    return _merge(x2, ya.reshape(seq, bsz * s5_w), yb.reshape(n, -1), gates,
                  w_pa.astype(BF16), w_pb.astype(BF16), w_o.astype(BF16),
                  ln1_g.astype(F32).reshape(1, d), ln1_b.astype(F32).reshape(1, d), bsz, seq, alpha, tm)


def _top_sorted(s, k, with_rank=False):
    rows = []
    cur = s
    rank = jnp.full(s.shape, float(k), F32) if with_rank else None
    for j in range(k):
        mx = jnp.max(cur, axis=0, keepdims=True)
        rows.append(mx)
        hit = cur == mx
        if with_rank:
            rank = jnp.where(hit, float(j), rank)
        if j + 1 < k:
            cur = jnp.where(hit, -jnp.inf, cur)
    return (rows, rank) if with_rank else rows


def _route_kernel(xb_ref, wq_ref, keys_ref, cnt_ref, p1_ref, r2_ref, p2_ref, s_scr, *, heads, topk):
    nk = keys_ref.shape[1]
    q_t = _dot_nt(wq_ref[...], xb_ref[...])
    for hp in range(2 * heads):
        s_scr[hp] = _dot(keys_ref[hp], q_t[hp * nk:(hp + 1) * nk, :].astype(BF16))

    def head(h, carry):
        s1 = s_scr[2 * h]
        s2 = s_scr[2 * h + 1]
        a = _top_sorted(s1, topk)
        b, rank2 = _top_sorted(s2, topk, with_rank=True)
        b_all = jnp.concatenate(b, axis=0)
        half_rows = topk // 2
        l_idx = lax.broadcasted_iota(jnp.int32, (half_rows, 1), 0)
        cands = [a[0] + b_all]
        for j in range(1, topk):
            cj = a[j] + b_all[:half_rows]
            cands.append(jnp.where(l_idx < topk // (j + 1), cj, -jnp.inf))
        cur = jnp.concatenate(cands, axis=0)
        m = a[0] + b[0]
        z = jnp.zeros_like(m)
        tau = m
        for r in range(topk):
            tau = jnp.max(cur, axis=0, keepdims=True)
            z = z + jnp.exp(tau - m)
            if r + 1 < topk:
                cur = jnp.where(cur == tau, -jnp.inf, cur)
        cnt = jnp.zeros(s1.shape, F32)
        for j in range(topk):
            cnt_j = jnp.sum(jnp.where(cands[j] >= tau, 1.0, 0.0), axis=0, keepdims=True)
            cnt = jnp.where(s1 == a[j], cnt_j, cnt)
        cnt_ref[h] = cnt.astype(BF16)
        p1_ref[h] = jnp.exp(s1 - a[0]).astype(BF16)
        r2_ref[h] = rank2.astype(BF16)
        p2_ref[h] = (jnp.exp(s2 - b[0]) / z).astype(BF16)
        return carry

    lax.fori_loop(0, heads, head, 0)


def _route(x1b, wq_t, keys, tt):
    n, d = x1b.shape
    hp, nk, _ = keys.shape
    heads = hp // 2
    kern = functools.partial(_route_kernel, heads=heads, topk=PEER_TOPK)
    out = lambda dt: jax.ShapeDtypeStruct((heads, nk, n), dt)
    ospec = pl.BlockSpec((heads, nk, tt), lambda i: (0, 0, i))
    return pl.pallas_call(
        kern,
        grid=(n // tt,),
        in_specs=[pl.BlockSpec((tt, d), lambda i: (i, 0)),
                  pl.BlockSpec(wq_t.shape, lambda i: (0, 0)),
                  pl.BlockSpec(keys.shape, lambda i: (0, 0, 0))],
        out_specs=[ospec, ospec, ospec, ospec],
        out_shape=[out(BF16), out(BF16), out(BF16), out(BF16)],
        scratch_shapes=[pltpu.VMEM((hp, nk, tt), F32)],
        compiler_params=_cparams(("parallel",)),
        name="route",
    )(x1b, wq_t, keys)


def _peer_kernel(xb_ref, x_ref, u_ref, vt_ref, cnt_ref, p1_ref, r2_ref, p2_ref, g_ref, b_ref,
                 out_ref, acc_ref, h_ref, a_ref, *, heads, nk, alpha):
    j = pl.program_id(1)
    tt = xb_ref.shape[0]
    n_i1 = u_ref.shape[0] // nk
    pk = BF16_SUBLANES
    tiles = nk // pk

    @pl.when(j == 0)
    def _():
        acc_ref[...] = jnp.zeros_like(acc_ref)

    h_ref[...] = _dot_nt(u_ref[...], xb_ref[...]).astype(BF16).reshape(n_i1 * tiles, pk, tt)

    i1_rows = pl.ds(pl.multiple_of(j * n_i1, n_i1), n_i1)
    cnt_t = [cnt_ref[h, i1_rows, :].astype(F32) for h in range(heads)]
    p1_t = [p1_ref[h, i1_rows, :].astype(F32) for h in range(heads)]
    for ii in range(n_i1):
        w = jnp.zeros((tiles, pk, tt), BF16)
        for h in range(heads):
            cnt = jnp.broadcast_to(cnt_t[h][ii:ii + 1], (pk, tt)).astype(BF16)
            p1 = jnp.broadcast_to(p1_t[h][ii:ii + 1], (pk, tt)).astype(BF16)
            r2 = r2_ref[h].reshape(tiles, pk, tt)
            p2 = p2_ref[h].reshape(tiles, pk, tt)
            w = w + jnp.where(r2 < cnt[None], p2, 0.0) * p1[None]
        a_ref[ii * tiles:(ii + 1) * tiles] = w * _gelu_tanh(h_ref[ii * tiles:(ii + 1) * tiles])
    acc_ref[...] += _dot(vt_ref[...], a_ref[...].reshape(n_i1 * nk, tt))

    @pl.when(j == pl.num_programs(1) - 1)
    def _():
        z = alpha * x_ref[...] + acc_ref[...].T
        out_ref[...] = _layer_norm_rows(z, g_ref[...], b_ref[...])


def _peer(x1, x1b, u_b, vt_b, routing, g_row, b_row, alpha, tt, chunk):
    n, d = x1.shape
    cnt, p1, r2, p2 = routing
    heads, nk, _ = cnt.shape
    n_exp = u_b.shape[0]
    assert chunk == BF16_SUBLANES * nk, "one grid step covers one packed sublane tile of i1 values"
    kern = functools.partial(_peer_kernel, heads=heads, nk=nk, alpha=alpha)
    rspec = pl.BlockSpec((heads, nk, tt), lambda i, j: (0, 0, i))
    packed = (chunk // BF16_SUBLANES, BF16_SUBLANES, tt)
    return pl.pallas_call(
        kern,
        grid=(n // tt, n_exp // chunk),
        in_specs=[pl.BlockSpec((tt, d), lambda i, j: (i, 0)),
                  pl.BlockSpec((tt, d), lambda i, j: (i, 0)),
                  pl.BlockSpec((chunk, d), lambda i, j: (j, 0)),
                  pl.BlockSpec((d, chunk), lambda i, j: (0, j)),
                  rspec, rspec, rspec, rspec,
                  pl.BlockSpec((1, d), lambda i, j: (0, 0)),
                  pl.BlockSpec((1, d), lambda i, j: (0, 0))],
        out_specs=pl.BlockSpec((tt, d), lambda i, j: (i, 0)),
        out_shape=jax.ShapeDtypeStruct((n, d), F32),
        scratch_shapes=[pltpu.VMEM((d, tt), F32), pltpu.VMEM(packed, BF16), pltpu.VMEM(packed, BF16)],
        compiler_params=_cparams(("parallel", "arbitrary")),
        name="peer",
    )(x1b, x1, u_b, vt_b, cnt, p1, r2, p2, g_row, b_row)


def _stage2(x1, x1b, w_pq, sub_keys, peer_u, peer_v, ln2_g, ln2_b, alpha):
    n, d = x1.shape
    heads, _, nk, half = sub_keys.shape
    tt = min(512, n)
    routing = _route(x1b, w_pq.T.astype(BF16), sub_keys.reshape(heads * 2, nk, half).astype(BF16), tt)
    return _peer(x1, x1b, peer_u.astype(BF16), peer_v.T.astype(BF16), routing,
                 ln2_g.astype(F32).reshape(1, d), ln2_b.astype(F32).reshape(1, d), alpha, tt, chunk=BF16_SUBLANES * nk)


def kernel(x, w_in, s5_lam_re, s5_lam_im, s5_b_re, s5_b_im, s5_c_re, s5_c_im, s5_d, s5_log_step, s5_w_glu,
           s5_b_glu, hgrn_lb_logits, hgrn_gnorm_w, w_pa, w_pb, w_o, ln1_g, ln1_b, peer_w_q, peer_sub_keys,
           peer_u, peer_v, ln2_g, ln2_b):
    depth = w_in.shape[0]
    alpha = (2.0 * depth) ** 0.25
    lb_all = jnp.cumsum(jax.nn.softmax(hgrn_lb_logits.astype(F32), axis=0), axis=0)[:depth]
    bsz, seq, d = x.shape
    for l in range(depth):
        x1, x1b = _stage1(x, w_in[l], s5_lam_re[l], s5_lam_im[l], s5_b_re[l], s5_b_im[l], s5_c_re[l], s5_c_im[l],
                          s5_d[l], s5_log_step[l], s5_w_glu[l], s5_b_glu[l], lb_all[l], hgrn_gnorm_w[l],
                          w_pa[l], w_pb[l], w_o[l], ln1_g[l], ln1_b[l], alpha)
        x = _stage2(x1, x1b, peer_w_q[l], peer_sub_keys[l], peer_u[l], peer_v[l], ln2_g[l], ln2_b[l],
                    alpha).reshape(bsz, seq, d)
    return x
```

```python
import functools
import math

import numpy as np
import jax
import jax.numpy as jnp
from jax import lax
from jax.experimental import pallas as pl
from jax.experimental.pallas import tpu as pltpu

F32 = jnp.float32
BF16 = jnp.bfloat16

V7X_VMEM_LIMIT_BYTES = 56 * 1024 * 1024
LANES = 128
SUBLANES = 8
BF16_SUBLANES = 16

S5_GROUP = 16
S5_STATE = 64
S5_GROUPS_PER_BLOCK = 8
HGRN_HEAD_DIM = 128
HGRN_CHUNK = 128
PEER_HEADS = 8
PEER_TOPK = 16
PEER_NKEYS = 128
LN_EPS = 1e-5
RMS_EPS = 1e-6


def _cparams(semantics):
    return pltpu.CompilerParams(dimension_semantics=semantics, vmem_limit_bytes=V7X_VMEM_LIMIT_BYTES)


def _dot(a, b):
    return jnp.dot(a, b, preferred_element_type=F32)


def _dot_nt(a, b):
    return lax.dot_general(a, b, (((1,), (1,)), ((), ())), preferred_element_type=F32)


def _dot_tn(a, b):
    return lax.dot_general(a, b, (((0,), (0,)), ((), ())), preferred_element_type=F32)


def _sigmoid(x):
    return 1.0 / (1.0 + jnp.exp(-x))


def _gelu_tanh(x):
    c = math.sqrt(2.0 / math.pi)
    return 0.5 * x * (1.0 + jnp.tanh(c * (x + 0.044715 * (x * x * x))))


def _layer_norm_rows(z, g, b):
    mu = jnp.mean(z, axis=-1, keepdims=True)
    zc = z - mu
    var = jnp.mean(zc * zc, axis=-1, keepdims=True)
    return zc * lax.rsqrt(var + LN_EPS) * g + b


def _proj_kernel(x_ref, w_ref, u_ref, hg_ref, gate_ref, *, s5_w, hg_w, col_chunk):
    xb = x_ref[...].astype(BF16)
    n_cols = w_ref.shape[1]
    for c0 in range(0, n_cols, col_chunk):
        p = _dot(xb, w_ref[:, c0:c0 + col_chunk]).astype(BF16)
        if c0 < s5_w:
            u_ref[:, c0:c0 + col_chunk] = p
        elif c0 < s5_w + hg_w:
            hg_ref[:, c0 - s5_w:c0 - s5_w + col_chunk] = p
        else:
            o = c0 - s5_w - hg_w
            gate_ref[:, o:o + col_chunk] = p


def _proj(x2, w_in_b, bsz, seq, s5_w, hg_w, gate_w, tm):
    n, d = x2.shape
    n_s = seq // tm
    kern = functools.partial(_proj_kernel, s5_w=s5_w, hg_w=hg_w, col_chunk=s5_w)
    return pl.pallas_call(
        kern,
        grid=(bsz, n_s),
        in_specs=[
            pl.BlockSpec((tm, d), lambda b, i: (b * n_s + i, 0)),
            pl.BlockSpec(w_in_b.shape, lambda b, i: (0, 0)),
        ],
        out_specs=[
            pl.BlockSpec((tm, s5_w), lambda b, i: (i, b)),
            pl.BlockSpec((tm, hg_w), lambda b, i: (b * n_s + i, 0)),
            pl.BlockSpec((tm, gate_w), lambda b, i: (b * n_s + i, 0)),
        ],
        out_shape=[
            jax.ShapeDtypeStruct((seq, bsz * s5_w), BF16),
            jax.ShapeDtypeStruct((n, hg_w), BF16),
            jax.ShapeDtypeStruct((n, gate_w), BF16),
        ],
        compiler_params=_cparams(("parallel", "parallel")),
        name="proj",
    )(x2, w_in_b)


def _s5_kernel(u_ref, bblk_ref, cblk_ref, lamr_ref, lami_ref, d_ref, wglu_ref, bglu_ref,
               y_ref, xs_ref, st_ref, *, bsz, tb, n_blk, half):
    @pl.when(pl.program_id(0) == 0)
    def _():
        st_ref[...] = jnp.zeros_like(st_ref)

    in_w = bblk_ref.shape[1]
    ys = []
    for j in range(n_blk):
        uj = u_ref[:, j * in_w:(j + 1) * in_w]
        xs_ref[...] = _dot(uj, bblk_ref[j])
        lr = jnp.broadcast_to(lamr_ref[j], (bsz, half))
        li = jnp.broadcast_to(lami_ref[j], (bsz, half))

        def body(t, carry, lr=lr, li=li):
            sr, si = carry
            rows = pl.ds(pl.multiple_of(t * bsz, bsz), bsz)
            nr = lr * sr - li * si + xs_ref[rows, :half]
            ni = lr * si + li * sr + xs_ref[rows, half:]
            xs_ref[rows, :half] = nr
            xs_ref[rows, half:] = ni
            return nr, ni

        sr, si = lax.fori_loop(0, tb, body, (st_ref[j, :, :half], st_ref[j, :, half:]), unroll=4)
        st_ref[j, :, :half] = sr
        st_ref[j, :, half:] = si
        ys.append(_dot(xs_ref[...].astype(BF16), cblk_ref[j]))
    y = jnp.concatenate(ys, axis=1) + d_ref[...] * u_ref[...].astype(F32)
    y = _gelu_tanh(y)
    z = _dot(y.astype(BF16), wglu_ref[...]) + bglu_ref[...]
    y_ref[...] = (y * _sigmoid(z)).astype(BF16)


def _s5_params(lam_re, lam_im, b_re, b_im, c_re, c_im, d_skip, log_step):
    g, p = lam_re.shape
    h = b_re.shape[-1]
    gb = S5_GROUPS_PER_BLOCK
    nb = g // gb
    lr, li = lam_re.astype(F32), lam_im.astype(F32)
    dt = jnp.exp(log_step.astype(F32))[:, None]
    mag = jnp.exp(lr * dt)
    lbr, lbi = mag * jnp.cos(li * dt), mag * jnp.sin(li * dt)
    den = lr * lr + li * li
    cr = ((lbr - 1.0) * lr + lbi * li) / den
    ci = (lbi * lr - (lbr - 1.0) * li) / den
    br, bi = b_re.astype(F32), b_im.astype(F32)
    bbar_r = cr[..., None] * br - ci[..., None] * bi
    bbar_i = cr[..., None] * bi + ci[..., None] * br
    eye = jnp.eye(gb, dtype=F32)

    def in_blk(m):
        m = m.reshape(nb, gb, p, h)
        return jnp.einsum('jgph,gk->jghkp', m, eye).reshape(nb, gb * h, gb * p)

    def out_blk(m):
        m = m.reshape(nb, gb, h, p)
        return jnp.einsum('jghp,gk->jgpkh', m, eye).reshape(nb, gb * p, gb * h)

    bblk = jnp.concatenate([in_blk(bbar_r), in_blk(bbar_i)], axis=2).astype(BF16)
    cblk = jnp.concatenate([out_blk(c_re.astype(F32)), out_blk(-c_im.astype(F32))], axis=1).astype(BF16)
    lamr = lbr.reshape(nb, 1, gb * p)
    lami = lbi.reshape(nb, 1, gb * p)
    return bblk, cblk, lamr, lami, d_skip.astype(F32).reshape(1, g * h)


def _s5(u_tm, params, w_glu_b, b_glu, bsz, seq, tb):
    bblk, cblk, lamr, lami, d_row = params
    n, w = u_tm.shape
    n_blk, in_w, two_half = bblk.shape
    half = two_half // 2
    rows = tb * bsz
    kern = functools.partial(_s5_kernel, bsz=bsz, tb=tb, n_blk=n_blk, half=half)
    const = lambda a: pl.BlockSpec(a.shape, lambda i: (0,) * a.ndim)
    return pl.pallas_call(
        kern,
        grid=(seq // tb,),
        in_specs=[pl.BlockSpec((rows, w), lambda i: (i, 0)),
                  const(bblk), const(cblk), const(lamr), const(lami), const(d_row),
                  const(w_glu_b), const(b_glu)],
        out_specs=pl.BlockSpec((rows, w), lambda i: (i, 0)),
        out_shape=jax.ShapeDtypeStruct((n, w), BF16),
        scratch_shapes=[pltpu.VMEM((rows, two_half), F32), pltpu.VMEM((n_blk, bsz, two_half), F32)],
        compiler_params=_cparams(("arbitrary",)),
        name="s5",
    )(u_tm, bblk, cblk, lamr, lami, d_row, w_glu_b, b_glu)


def _hgrn_tables(c):
    n_lev = int(math.log2(c))
    t = np.arange(c)[:, None]
    r = np.arange(c)[None, :]
    sel = []
    masks = []
    for lev in range(n_lev):
        w = c >> (lev + 1)
        pos = t % (2 * w)
        a = t - pos + w - 1
        upper = pos >= w
        sel.append(np.where(upper, (r > a) & (r <= t), (r > t) & (r <= a)))
        tt, ss = np.arange(c)[:, None], np.arange(c)[None, :]
        masks.append((tt // (2 * w) == ss // (2 * w)) & (tt % (2 * w) >= w) & (ss % (2 * w) < w))
    sel.append(r <= t)
    sel.append(r > t)
    masks.append(np.eye(c, dtype=bool))
    sel = np.concatenate(sel, axis=0).astype(np.float32)
    masks = np.stack(masks, axis=0).astype(np.float32)
    return jnp.asarray(sel, BF16), jnp.asarray(masks, F32)


def _hgrn_kernel(hg_ref, sel_ref, mask_ref, lb_ref, gw_ref, y_ref, st_ref, *, c, n_chunks, width, heads):
    @pl.when(pl.program_id(1) == 0)
    def _():
        st_ref[...] = jnp.zeros_like(st_ref)

    n_lev = mask_ref.shape[0] - 1
    dh = width // heads
    lb = lb_ref[...]
    gw = gw_ref[...]

    def chunk(ci, carry):
        rows = pl.ds(pl.multiple_of(ci * c, c), c)
        q = hg_ref[0, rows, 0:width].astype(F32)
        fp = hg_ref[0, rows, width:2 * width].astype(F32)
        v = hg_ref[0, rows, 2 * width:3 * width]
        g = hg_ref[0, rows, 3 * width:4 * width].astype(F32)
        f = lb + (1.0 - lb) * _sigmoid(fp)
        lf = jnp.log(f)
        k = 1.0 - f
        qf = q * _sigmoid(q)
        h1 = lf.astype(BF16)
        h2 = (lf - h1.astype(F32)).astype(BF16)
        sel = sel_ref[...]
        e_all = jnp.exp(_dot(sel, h1) + _dot(sel, h2))
        e_cum = e_all[n_lev * c:(n_lev + 1) * c]
        e_suf = e_all[(n_lev + 1) * c:(n_lev + 2) * c]
        outs = []
        for hd in range(heads):
            cols = slice(hd * dh, (hd + 1) * dh)
            qh, kh = qf[:, cols], k[:, cols]
            vh = v[:, cols]
            sc = mask_ref[n_lev] * _dot_nt(qh.astype(BF16), kh.astype(BF16))
            for lev in range(n_lev):
                el = e_all[lev * c:(lev + 1) * c, cols]
                sc = sc + mask_ref[lev] * _dot_nt((qh * el).astype(BF16), (kh * el).astype(BF16))
            st = st_ref[hd]
            o = _dot_nt((qh * e_cum[:, cols]).astype(BF16), st.astype(BF16)) + _dot(sc.astype(BF16), vh)
            e_tot = e_cum[c - 1:c, cols]
            st_ref[hd] = st * e_tot + _dot_tn(vh, (kh * e_suf[:, cols]).astype(BF16))
            o = o * lax.rsqrt(jnp.mean(o * o, axis=-1, keepdims=True) + RMS_EPS)
            outs.append(o)
        o = jnp.concatenate(outs, axis=1) * gw * (g * _sigmoid(g))
        y_ref[0, rows, :] = o.astype(BF16)
        return carry

    lax.fori_loop(0, n_chunks, chunk, 0, unroll=2)


def _hgrn(hg3, lb_row, gw_row, heads, tbh):
    bsz, seq, w4 = hg3.shape
    width = w4 // 4
    c = HGRN_CHUNK
    sel, masks = _hgrn_tables(c)
    kern = functools.partial(_hgrn_kernel, c=c, n_chunks=tbh // c, width=width, heads=heads)
    const = lambda a: pl.BlockSpec(a.shape, lambda b, i: (0,) * a.ndim)
    return pl.pallas_call(
        kern,
        grid=(bsz, seq // tbh),
        in_specs=[pl.BlockSpec((1, tbh, w4), lambda b, i: (b, i, 0)),
                  const(sel), const(masks), const(lb_row), const(gw_row)],
        out_specs=pl.BlockSpec((1, tbh, width), lambda b, i: (b, i, 0)),
        out_shape=jax.ShapeDtypeStruct((bsz, seq, width), BF16),
        scratch_shapes=[pltpu.VMEM((heads, width // heads, width // heads), F32)],
        compiler_params=_cparams(("parallel", "arbitrary")),
        name="hgrn",
    )(hg3, sel, masks, lb_row, gw_row)


def _merge_kernel(x_ref, ya_ref, yb_ref, gate_ref, wpa_ref, wpb_ref, wo_ref, g_ref, b_ref,
                  x1_ref, x1b_ref, *, alpha, d):
    ga = gate_ref[:, :d].astype(F32)
    gb = gate_ref[:, d:].astype(F32)
    merged = _sigmoid(ga) * _dot(ya_ref[...], wpa_ref[...]) + _sigmoid(gb) * _dot(yb_ref[...], wpb_ref[...])
    mix = _dot(merged.astype(BF16), wo_ref[...])
    x1 = _layer_norm_rows(alpha * x_ref[...] + mix, g_ref[...], b_ref[...])
    x1_ref[...] = x1
    x1b_ref[...] = x1.astype(BF16)


def _merge(x2, ya_tm, yb2, gates, wpa_b, wpb_b, wo_b, g_row, b_row, bsz, seq, alpha, tm):
    n, d = x2.shape
    w = yb2.shape[1]
    n_s = seq // tm
    kern = functools.partial(_merge_kernel, alpha=alpha, d=d)
    const = lambda a: pl.BlockSpec(a.shape, lambda b, i: (0,) * a.ndim)
    row = lambda cols: pl.BlockSpec((tm, cols), lambda b, i: (b * n_s + i, 0))
    return pl.pallas_call(
        kern,
        grid=(bsz, n_s),
        in_specs=[row(d),
                  pl.BlockSpec((tm, w), lambda b, i: (i, b)),
                  row(w), row(2 * d),
                  const(wpa_b), const(wpb_b), const(wo_b), const(g_row), const(b_row)],
        out_specs=[row(d), row(d)],
        out_shape=[jax.ShapeDtypeStruct((n, d), F32), jax.ShapeDtypeStruct((n, d), BF16)],
        compiler_params=_cparams(("parallel", "parallel")),
        name="merge",
    )(x2, ya_tm, yb2, gates, wpa_b, wpb_b, wo_b, g_row, b_row)


def _stage1(x, w_in, s5_lam_re, s5_lam_im, s5_b_re, s5_b_im, s5_c_re, s5_c_im, s5_d, s5_log_step,
            s5_w_glu, s5_b_glu, lb, gnorm_w, w_pa, w_pb, w_o, ln1_g, ln1_b, alpha):
    bsz, seq, d = x.shape
    n = bsz * seq
    s5_w = s5_w_glu.shape[0]
    hg_w = 4 * gnorm_w.shape[0]
    gate_w = 2 * d
    tm = min(512, seq)
    x2 = x.reshape(n, d)
    u_tm, hg, gates = _proj(x2, w_in.astype(BF16), bsz, seq, s5_w, hg_w, gate_w, tm)
    s5p = _s5_params(s5_lam_re, s5_lam_im, s5_b_re, s5_b_im, s5_c_re, s5_c_im, s5_d, s5_log_step)
    ya = _s5(u_tm.reshape(seq * bsz, s5_w), s5p, s5_w_glu.astype(BF16),
             s5_b_glu.astype(F32).reshape(1, s5_w), bsz, seq, tb=min(128, seq))
    yb = _hgrn(hg.reshape(bsz, seq, hg_w), lb.reshape(1, -1), gnorm_w.astype(F32).reshape(1, -1),
               heads=gnorm_w.shape[0] // HGRN_HEAD_DIM, tbh=min(512, seq))
    return _merge(x2, ya.reshape(seq, bsz * s5_w), yb.reshape(n, -1), gates,
                  w_pa.astype(BF16), w_pb.astype(BF16), w_o.astype(BF16),
                  ln1_g.astype(F32).reshape(1, d), ln1_b.astype(F32).reshape(1, d), bsz, seq, alpha, tm)


RANK_BASE = 2.0 ** 100


def _top_sorted(s, k):
    rows = []
    cur = s
    for j in range(k):
        mx = jnp.max(cur, axis=0, keepdims=True)
        rows.append(mx)
        cur = jnp.where(cur == mx, -(j + 1) * RANK_BASE, cur)
    rank = jnp.where(cur < -0.5 * RANK_BASE, cur * (-1.0 / RANK_BASE) - 1.0, float(k))
    return rows, rank


def _route_kernel(xb_ref, wq_ref, keys_ref, cnt_ref, p1_ref, r2_ref, p2_ref, s_scr, *, heads, topk):
    nk = keys_ref.shape[1]
    q_t = _dot_nt(wq_ref[...], xb_ref[...])
    for hp in range(2 * heads):
        s_scr[hp] = _dot(keys_ref[hp], q_t[hp * nk:(hp + 1) * nk, :].astype(BF16))

    def head(h, carry):
        s1 = s_scr[2 * h]
        s2 = s_scr[2 * h + 1]
        a, rank1 = _top_sorted(s1, topk)
        b, rank2 = _top_sorted(s2, topk)
        a_all = jnp.concatenate(a, axis=0)
        b_all = jnp.concatenate(b, axis=0)
        half_rows = topk // 2
        j_idx = lax.broadcasted_iota(jnp.int32, (half_rows, 1), 0)
        cands = [a_all + b[0]]
        for l in range(1, half_rows):
            cands.append(jnp.where(j_idx < topk // (l + 1), a_all[:half_rows] + b[l], -jnp.inf))
        cands.append(a[0] + b_all[half_rows:])
        cur = jnp.concatenate(cands, axis=0)
        m = a[0] + b[0]
        z = jnp.zeros_like(m)
        tau = m
        for r in range(topk):
            tau = jnp.max(cur, axis=0, keepdims=True)
            z = z + jnp.exp(tau - m)
            if r + 1 < topk:
                cur = jnp.where(cur == tau, -jnp.inf, cur)
        hit = lambda c: jnp.where(c >= tau, 1.0, 0.0)
        cnt_lo = hit(cands[0][:half_rows])
        for l in range(1, half_rows):
            cnt_lo = cnt_lo + hit(cands[l])
        cnt_0 = cnt_lo[0:1] + jnp.sum(hit(cands[half_rows]), axis=0, keepdims=True)
        cnt_rows = jnp.concatenate([cnt_lo, hit(cands[0][half_rows:])], axis=0)
        cnt = jnp.zeros(s1.shape, F32)
        for c in range(half_rows):
            ranks_with = jnp.sum(jnp.where(cnt_rows > float(c), 1.0, 0.0), axis=0, keepdims=True)
            cnt = jnp.where(rank1 < ranks_with, float(c + 1), cnt)
        cnt = jnp.where(rank1 == 0.0, jnp.maximum(cnt, cnt_0), cnt)
        cnt_ref[h] = cnt.astype(BF16)
        p1_ref[h] = jnp.exp(s1 - a[0]).astype(BF16)
        r2_ref[h] = rank2.astype(BF16)
        p2_ref[h] = (jnp.exp(s2 - b[0]) / z).astype(BF16)
        return carry

    lax.fori_loop(0, heads, head, 0)


def _route(x1b, wq_t, keys, tt):
    n, d = x1b.shape
    hp, nk, _ = keys.shape
    heads = hp // 2
    kern = functools.partial(_route_kernel, heads=heads, topk=PEER_TOPK)
    out = lambda dt: jax.ShapeDtypeStruct((heads, nk, n), dt)
    ospec = pl.BlockSpec((heads, nk, tt), lambda i: (0, 0, i))
    return pl.pallas_call(
        kern,
        grid=(n // tt,),
        in_specs=[pl.BlockSpec((tt, d), lambda i: (i, 0)),
                  pl.BlockSpec(wq_t.shape, lambda i: (0, 0)),
                  pl.BlockSpec(keys.shape, lambda i: (0, 0, 0))],
        out_specs=[ospec, ospec, ospec, ospec],
        out_shape=[out(BF16), out(BF16), out(BF16), out(BF16)],
        scratch_shapes=[pltpu.VMEM((hp, nk, tt), F32)],
        compiler_params=_cparams(("parallel",)),
        name="route",
    )(x1b, wq_t, keys)


def _peer_kernel(xb_ref, x_ref, u_ref, vt_ref, cnt_ref, p1_ref, r2_ref, p2_ref, g_ref, b_ref,
                 out_ref, acc_ref, h_ref, a_ref, *, heads, nk, alpha):
    j = pl.program_id(1)
    tt = xb_ref.shape[0]
    n_i1 = u_ref.shape[0] // nk
    pk = BF16_SUBLANES
    tiles = nk // pk

    @pl.when(j == 0)
    def _():
        acc_ref[...] = jnp.zeros_like(acc_ref)

    h_ref[...] = _dot_nt(u_ref[...], xb_ref[...]).astype(BF16).reshape(n_i1 * tiles, pk, tt)

    i1_rows = pl.ds(pl.multiple_of(j * n_i1, n_i1), n_i1)
    cnt_t = [cnt_ref[h, i1_rows, :].astype(F32) for h in range(heads)]
    p1_t = [p1_ref[h, i1_rows, :].astype(F32) for h in range(heads)]
    for ii in range(n_i1):
        w = jnp.zeros((tiles, pk, tt), BF16)
        for h in range(heads):
            cnt = jnp.broadcast_to(cnt_t[h][ii:ii + 1], (pk, tt)).astype(BF16)
            p1 = jnp.broadcast_to(p1_t[h][ii:ii + 1], (pk, tt)).astype(BF16)
            r2 = r2_ref[h].reshape(tiles, pk, tt)
            p2 = p2_ref[h].reshape(tiles, pk, tt)
            w = w + jnp.where(r2 < cnt[None], p2, 0.0) * p1[None]
        a_ref[ii * tiles:(ii + 1) * tiles] = w * _gelu_tanh(h_ref[ii * tiles:(ii + 1) * tiles])
    acc_ref[...] += _dot(vt_ref[...], a_ref[...].reshape(n_i1 * nk, tt))

    @pl.when(j == pl.num_programs(1) - 1)
    def _():
        z = alpha * x_ref[...] + acc_ref[...].T
        out_ref[...] = _layer_norm_rows(z, g_ref[...], b_ref[...])


def _peer(x1, x1b, u_b, vt_b, routing, g_row, b_row, alpha, tt, chunk):
    n, d = x1.shape
    cnt, p1, r2, p2 = routing
    heads, nk, _ = cnt.shape
    n_exp = u_b.shape[0]
    assert chunk == BF16_SUBLANES * nk, "one grid step covers one packed sublane tile of i1 values"
    kern = functools.partial(_peer_kernel, heads=heads, nk=nk, alpha=alpha)
    rspec = pl.BlockSpec((heads, nk, tt), lambda i, j: (0, 0, i))
    packed = (chunk // BF16_SUBLANES, BF16_SUBLANES, tt)
    return pl.pallas_call(
        kern,
        grid=(n // tt, n_exp // chunk),
        in_specs=[pl.BlockSpec((tt, d), lambda i, j: (i, 0)),
                  pl.BlockSpec((tt, d), lambda i, j: (i, 0)),
                  pl.BlockSpec((chunk, d), lambda i, j: (j, 0)),
                  pl.BlockSpec((d, chunk), lambda i, j: (0, j)),
                  rspec, rspec, rspec, rspec,
                  pl.BlockSpec((1, d), lambda i, j: (0, 0)),
                  pl.BlockSpec((1, d), lambda i, j: (0, 0))],
        out_specs=pl.BlockSpec((tt, d), lambda i, j: (i, 0)),
        out_shape=jax.ShapeDtypeStruct((n, d), F32),
        scratch_shapes=[pltpu.VMEM((d, tt), F32), pltpu.VMEM(packed, BF16), pltpu.VMEM(packed, BF16)],
        compiler_params=_cparams(("parallel", "arbitrary")),
        name="peer",
    )(x1b, x1, u_b, vt_b, cnt, p1, r2, p2, g_row, b_row)


def _stage2(x1, x1b, w_pq, sub_keys, peer_u, peer_v, ln2_g, ln2_b, alpha):
    n, d = x1.shape
    heads, _, nk, half = sub_keys.shape
    tt = min(512, n)
    routing = _route(x1b, w_pq.T.astype(BF16), sub_keys.reshape(heads * 2, nk, half).astype(BF16), tt)
    return _peer(x1, x1b, peer_u.astype(BF16), peer_v.T.astype(BF16), routing,
                 ln2_g.astype(F32).reshape(1, d), ln2_b.astype(F32).reshape(1, d), alpha, tt, chunk=BF16_SUBLANES * nk)


def kernel(x, w_in, s5_lam_re, s5_lam_im, s5_b_re, s5_b_im, s5_c_re, s5_c_im, s5_d, s5_log_step, s5_w_glu,
           s5_b_glu, hgrn_lb_logits, hgrn_gnorm_w, w_pa, w_pb, w_o, ln1_g, ln1_b, peer_w_q, peer_sub_keys,
           peer_u, peer_v, ln2_g, ln2_b):
    depth = w_in.shape[0]
    alpha = (2.0 * depth) ** 0.25
    lb_all = jnp.cumsum(jax.nn.softmax(hgrn_lb_logits.astype(F32), axis=0), axis=0)[:depth]
    bsz, seq, d = x.shape
    for l in range(depth):
        x1, x1b = _stage1(x, w_in[l], s5_lam_re[l], s5_lam_im[l], s5_b_re[l], s5_b_im[l], s5_c_re[l], s5_c_im[l],
                          s5_d[l], s5_log_step[l], s5_w_glu[l], s5_b_glu[l], lb_all[l], hgrn_gnorm_w[l],
                          w_pa[l], w_pb[l], w_o[l], ln1_g[l], ln1_b[l], alpha)
        x = _stage2(x1, x1b, peer_w_q[l], peer_sub_keys[l], peer_u[l], peer_v[l], ln2_g[l], ln2_b[l],
                    alpha).reshape(bsz, seq, d)
    return x
```

```python
import functools
import math

import numpy as np
import jax
import jax.numpy as jnp
from jax import lax
from jax.experimental import pallas as pl
from jax.experimental.pallas import tpu as pltpu

F32 = jnp.float32
BF16 = jnp.bfloat16

V7X_VMEM_LIMIT_BYTES = 56 * 1024 * 1024
LANES = 128
SUBLANES = 8
BF16_SUBLANES = 16

S5_GROUP = 16
S5_STATE = 64
S5_GROUPS_PER_BLOCK = 8
HGRN_HEAD_DIM = 128
HGRN_CHUNK = 128
PEER_HEADS = 8
PEER_TOPK = 16
PEER_NKEYS = 128
LN_EPS = 1e-5
RMS_EPS = 1e-6


def _cparams(semantics):
    return pltpu.CompilerParams(dimension_semantics=semantics, vmem_limit_bytes=V7X_VMEM_LIMIT_BYTES)


def _dot(a, b):
    return jnp.dot(a, b, preferred_element_type=F32)


def _dot_nt(a, b):
    return lax.dot_general(a, b, (((1,), (1,)), ((), ())), preferred_element_type=F32)


def _dot_tn(a, b):
    return lax.dot_general(a, b, (((0,), (0,)), ((), ())), preferred_element_type=F32)


def _sigmoid(x):
    return 1.0 / (1.0 + jnp.exp(-x))


def _gelu_tanh(x):
    c = math.sqrt(2.0 / math.pi)
    return 0.5 * x * (1.0 + jnp.tanh(c * (x + 0.044715 * (x * x * x))))


def _layer_norm_rows(z, g, b):
    mu = jnp.mean(z, axis=-1, keepdims=True)
    zc = z - mu
    var = jnp.mean(zc * zc, axis=-1, keepdims=True)
    return zc * lax.rsqrt(var + LN_EPS) * g + b


def _proj_kernel(x_ref, w_ref, u_ref, hg_ref, gate_ref, *, s5_w, hg_w, col_chunk):
    xb = x_ref[...].astype(BF16)
    n_cols = w_ref.shape[1]
    for c0 in range(0, n_cols, col_chunk):
        p = _dot(xb, w_ref[:, c0:c0 + col_chunk]).astype(BF16)
        if c0 < s5_w:
            u_ref[:, c0:c0 + col_chunk] = p
        elif c0 < s5_w + hg_w:
            hg_ref[:, c0 - s5_w:c0 - s5_w + col_chunk] = p
        else:
            o = c0 - s5_w - hg_w
            gate_ref[:, o:o + col_chunk] = p


def _proj(x2, w_in_b, bsz, seq, s5_w, hg_w, gate_w, tm):
    n, d = x2.shape
    n_s = seq // tm
    kern = functools.partial(_proj_kernel, s5_w=s5_w, hg_w=hg_w, col_chunk=s5_w)
    return pl.pallas_call(
        kern,
        grid=(bsz, n_s),
        in_specs=[
            pl.BlockSpec((tm, d), lambda b, i: (b * n_s + i, 0)),
            pl.BlockSpec(w_in_b.shape, lambda b, i: (0, 0)),
        ],
        out_specs=[
            pl.BlockSpec((tm, s5_w), lambda b, i: (i, b)),
            pl.BlockSpec((tm, hg_w), lambda b, i: (b * n_s + i, 0)),
            pl.BlockSpec((tm, gate_w), lambda b, i: (b * n_s + i, 0)),
        ],
        out_shape=[
            jax.ShapeDtypeStruct((seq, bsz * s5_w), BF16),
            jax.ShapeDtypeStruct((n, hg_w), BF16),
            jax.ShapeDtypeStruct((n, gate_w), BF16),
        ],
        compiler_params=_cparams(("parallel", "parallel")),
        name="proj",
    )(x2, w_in_b)


def _s5_kernel(u_ref, bblk_ref, cblk_ref, lamr_ref, lami_ref, d_ref, wglu_ref, bglu_ref,
               y_ref, xs_ref, st_ref, *, bsz, tb, n_blk, half):
    @pl.when(pl.program_id(0) == 0)
    def _():
        st_ref[...] = jnp.zeros_like(st_ref)

    in_w = bblk_ref.shape[1]
    ys = []
    for j in range(n_blk):
        uj = u_ref[:, j * in_w:(j + 1) * in_w]
        xs_ref[...] = _dot(uj, bblk_ref[j])
        lr = jnp.broadcast_to(lamr_ref[j], (bsz, half))
        li = jnp.broadcast_to(lami_ref[j], (bsz, half))

        def body(t, carry, lr=lr, li=li):
            sr, si = carry
            rows = pl.ds(pl.multiple_of(t * bsz, bsz), bsz)
            nr = lr * sr - li * si + xs_ref[rows, :half]
            ni = lr * si + li * sr + xs_ref[rows, half:]
            xs_ref[rows, :half] = nr
            xs_ref[rows, half:] = ni
            return nr, ni

        sr, si = lax.fori_loop(0, tb, body, (st_ref[j, :, :half], st_ref[j, :, half:]), unroll=True)
        st_ref[j, :, :half] = sr
        st_ref[j, :, half:] = si
        ys.append(_dot(xs_ref[...].astype(BF16), cblk_ref[j]))
    y = jnp.concatenate(ys, axis=1) + d_ref[...] * u_ref[...].astype(F32)
    y = _gelu_tanh(y)
    z = _dot(y.astype(BF16), wglu_ref[...]) + bglu_ref[...]
    y_ref[...] = (y * _sigmoid(z)).astype(BF16)


def _s5_params(lam_re, lam_im, b_re, b_im, c_re, c_im, d_skip, log_step):
    g, p = lam_re.shape
    h = b_re.shape[-1]
    gb = S5_GROUPS_PER_BLOCK
    nb = g // gb
    lr, li = lam_re.astype(F32), lam_im.astype(F32)
    dt = jnp.exp(log_step.astype(F32))[:, None]
    mag = jnp.exp(lr * dt)
    lbr, lbi = mag * jnp.cos(li * dt), mag * jnp.sin(li * dt)
    den = lr * lr + li * li
    cr = ((lbr - 1.0) * lr + lbi * li) / den
    ci = (lbi * lr - (lbr - 1.0) * li) / den
    br, bi = b_re.astype(F32), b_im.astype(F32)
    bbar_r = cr[..., None] * br - ci[..., None] * bi
    bbar_i = cr[..., None] * bi + ci[..., None] * br
    eye = jnp.eye(gb, dtype=F32)

    def in_blk(m):
        m = m.reshape(nb, gb, p, h)
        return jnp.einsum('jgph,gk->jghkp', m, eye).reshape(nb, gb * h, gb * p)

    def out_blk(m):
        m = m.reshape(nb, gb, h, p)
        return jnp.einsum('jghp,gk->jgpkh', m, eye).reshape(nb, gb * p, gb * h)

    bblk = jnp.concatenate([in_blk(bbar_r), in_blk(bbar_i)], axis=2).astype(BF16)
    cblk = jnp.concatenate([out_blk(c_re.astype(F32)), out_blk(-c_im.astype(F32))], axis=1).astype(BF16)
    lamr = lbr.reshape(nb, 1, gb * p)
    lami = lbi.reshape(nb, 1, gb * p)
    return bblk, cblk, lamr, lami, d_skip.astype(F32).reshape(1, g * h)


def _s5(u_tm, params, w_glu_b, b_glu, bsz, seq, tb):
    bblk, cblk, lamr, lami, d_row = params
    n, w = u_tm.shape
    n_blk, in_w, two_half = bblk.shape
    half = two_half // 2
    rows = tb * bsz
    kern = functools.partial(_s5_kernel, bsz=bsz, tb=tb, n_blk=n_blk, half=half)
    const = lambda a: pl.BlockSpec(a.shape, lambda i: (0,) * a.ndim)
    return pl.pallas_call(
        kern,
        grid=(seq // tb,),
        in_specs=[pl.BlockSpec((rows, w), lambda i: (i, 0)),
                  const(bblk), const(cblk), const(lamr), const(lami), const(d_row),
                  const(w_glu_b), const(b_glu)],
        out_specs=pl.BlockSpec((rows, w), lambda i: (i, 0)),
        out_shape=jax.ShapeDtypeStruct((n, w), BF16),
        scratch_shapes=[pltpu.VMEM((rows, two_half), F32), pltpu.VMEM((n_blk, bsz, two_half), F32)],
        compiler_params=_cparams(("arbitrary",)),
        name="s5",
    )(u_tm, bblk, cblk, lamr, lami, d_row, w_glu_b, b_glu)


def _hgrn_tables(c):
    n_lev = int(math.log2(c))
    t = np.arange(c)[:, None]
    r = np.arange(c)[None, :]
    sel = []
    masks = []
    for lev in range(n_lev):
        w = c >> (lev + 1)
        pos = t % (2 * w)
        a = t - pos + w - 1
        upper = pos >= w
        sel.append(np.where(upper, (r > a) & (r <= t), (r > t) & (r <= a)))
        tt, ss = np.arange(c)[:, None], np.arange(c)[None, :]
        masks.append((tt // (2 * w) == ss // (2 * w)) & (tt % (2 * w) >= w) & (ss % (2 * w) < w))
    sel.append(r <= t)
    sel.append(r > t)
    masks.append(np.eye(c, dtype=bool))
    sel = np.concatenate(sel, axis=0).astype(np.float32)
    sel = np.concatenate([sel, sel], axis=1)
    masks = np.stack(masks, axis=0).astype(np.float32)
    return jnp.asarray(sel, BF16), jnp.asarray(masks, F32)


def _hgrn_kernel(hg_ref, sel_ref, mask_ref, lb_ref, gw_ref, y_ref, st_ref, *, c, n_chunks, width, heads):
    @pl.when(pl.program_id(1) == 0)
    def _():
        st_ref[...] = jnp.zeros_like(st_ref)

    n_lev = mask_ref.shape[0] - 1
    dh = width // heads
    lb = lb_ref[...]
    gw = gw_ref[...]

    def chunk(ci, carry):
        rows = pl.ds(pl.multiple_of(ci * c, c), c)
        q = hg_ref[0, rows, 0:width].astype(F32)
        fp = hg_ref[0, rows, width:2 * width].astype(F32)
        v = hg_ref[0, rows, 2 * width:3 * width]
        g = hg_ref[0, rows, 3 * width:4 * width].astype(F32)
        f = lb + (1.0 - lb) * _sigmoid(fp)
        lf = jnp.log(f)
        k = 1.0 - f
        qf = q * _sigmoid(q)
        h1 = lf.astype(BF16)
        h2 = (lf - h1.astype(F32)).astype(BF16)
        e_all = jnp.exp(_dot(sel_ref[...], jnp.concatenate([h1, h2], axis=0)))
        e_cum = e_all[n_lev * c:(n_lev + 1) * c]
        e_suf = e_all[(n_lev + 1) * c:(n_lev + 2) * c]
        outs = []
        for hd in range(heads):
            cols = slice(hd * dh, (hd + 1) * dh)
            qh, kh = qf[:, cols], k[:, cols]
            vh = v[:, cols]
            sc = mask_ref[n_lev] * _dot_nt(qh.astype(BF16), kh.astype(BF16))
            for lev in range(n_lev):
                el = e_all[lev * c:(lev + 1) * c, cols]
                sc = sc + mask_ref[lev] * _dot_nt((qh * el).astype(BF16), (kh * el).astype(BF16))
            st = st_ref[hd]
            o = _dot_nt((qh * e_cum[:, cols]).astype(BF16), st.astype(BF16)) + _dot(sc.astype(BF16), vh)
            e_tot = e_cum[c - 1:c, cols]
            st_ref[hd] = st * e_tot + _dot_tn(vh, (kh * e_suf[:, cols]).astype(BF16))
            o = o * lax.rsqrt(jnp.mean(o * o, axis=-1, keepdims=True) + RMS_EPS)
            outs.append(o)
        o = jnp.concatenate(outs, axis=1) * gw * (g * _sigmoid(g))
        y_ref[0, rows, :] = o.astype(BF16)
        return carry

    lax.fori_loop(0, n_chunks, chunk, 0, unroll=2)


def _hgrn(hg3, lb_row, gw_row, heads, tbh):
    bsz, seq, w4 = hg3.shape
    width = w4 // 4
    c = HGRN_CHUNK
    sel, masks = _hgrn_tables(c)
    kern = functools.partial(_hgrn_kernel, c=c, n_chunks=tbh // c, width=width, heads=heads)
    const = lambda a: pl.BlockSpec(a.shape, lambda b, i: (0,) * a.ndim)
    return pl.pallas_call(
        kern,
        grid=(bsz, seq // tbh),
        in_specs=[pl.BlockSpec((1, tbh, w4), lambda b, i: (b, i, 0)),
                  const(sel), const(masks), const(lb_row), const(gw_row)],
        out_specs=pl.BlockSpec((1, tbh, width), lambda b, i: (b, i, 0)),
        out_shape=jax.ShapeDtypeStruct((bsz, seq, width), BF16),
        scratch_shapes=[pltpu.VMEM((heads, width // heads, width // heads), F32)],
        compiler_params=_cparams(("parallel", "arbitrary")),
        name="hgrn",
    )(hg3, sel, masks, lb_row, gw_row)


def _merge_kernel(x_ref, ya_ref, yb_ref, gate_ref, wpa_ref, wpb_ref, wo_ref, g_ref, b_ref,
                  x1_ref, x1b_ref, *, alpha, d):
    ga = gate_ref[:, :d].astype(F32)
    gb = gate_ref[:, d:].astype(F32)
    merged = _sigmoid(ga) * _dot(ya_ref[...], wpa_ref[...]) + _sigmoid(gb) * _dot(yb_ref[...], wpb_ref[...])
    mix = _dot(merged.astype(BF16), wo_ref[...])
    x1 = _layer_norm_rows(alpha * x_ref[...] + mix, g_ref[...], b_ref[...])
    x1_ref[...] = x1
    x1b_ref[...] = x1.astype(BF16)


def _merge(x2, ya_tm, yb2, gates, wpa_b, wpb_b, wo_b, g_row, b_row, bsz, seq, alpha, tm):
    n, d = x2.shape
    w = yb2.shape[1]
    n_s = seq // tm
    kern = functools.partial(_merge_kernel, alpha=alpha, d=d)
    const = lambda a: pl.BlockSpec(a.shape, lambda b, i: (0,) * a.ndim)
    row = lambda cols: pl.BlockSpec((tm, cols), lambda b, i: (b * n_s + i, 0))
    return pl.pallas_call(
        kern,
        grid=(bsz, n_s),
        in_specs=[row(d),
                  pl.BlockSpec((tm, w), lambda b, i: (i, b)),
                  row(w), row(2 * d),
                  const(wpa_b), const(wpb_b), const(wo_b), const(g_row), const(b_row)],
        out_specs=[row(d), row(d)],
        out_shape=[jax.ShapeDtypeStruct((n, d), F32), jax.ShapeDtypeStruct((n, d), BF16)],
        compiler_params=_cparams(("parallel", "parallel")),
        name="merge",
    )(x2, ya_tm, yb2, gates, wpa_b, wpb_b, wo_b, g_row, b_row)


def _stage1(x, w_in, s5_lam_re, s5_lam_im, s5_b_re, s5_b_im, s5_c_re, s5_c_im, s5_d, s5_log_step,
            s5_w_glu, s5_b_glu, lb, gnorm_w, w_pa, w_pb, w_o, ln1_g, ln1_b, alpha):
    bsz, seq, d = x.shape
    n = bsz * seq
    s5_w = s5_w_glu.shape[0]
    hg_w = 4 * gnorm_w.shape[0]
    gate_w = 2 * d
    tm = min(512, seq)
    x2 = x.reshape(n, d)
    u_tm, hg, gates = _proj(x2, w_in.astype(BF16), bsz, seq, s5_w, hg_w, gate_w, tm)
    s5p = _s5_params(s5_lam_re, s5_lam_im, s5_b_re, s5_b_im, s5_c_re, s5_c_im, s5_d, s5_log_step)
    ya = _s5(u_tm.reshape(seq * bsz, s5_w), s5p, s5_w_glu.astype(BF16),
             s5_b_glu.astype(F32).reshape(1, s5_w), bsz, seq, tb=min(128, seq))
    yb = _hgrn(hg.reshape(bsz, seq, hg_w), lb.reshape(1, -1), gnorm_w.astype(F32).reshape(1, -1),
               heads=gnorm_w.shape[0] // HGRN_HEAD_DIM, tbh=min(512, seq))
    return _merge(x2, ya.reshape(seq, bsz * s5_w), yb.reshape(n, -1), gates,
                  w_pa.astype(BF16), w_pb.astype(BF16), w_o.astype(BF16),
                  ln1_g.astype(F32).reshape(1, d), ln1_b.astype(F32).reshape(1, d), bsz, seq, alpha, tm)


RANK_BASE = 2.0 ** 100


def _top_sorted(s, k):
    rows = []
    cur = s
    for j in range(k):
        mx = jnp.max(cur, axis=0, keepdims=True)
        rows.append(mx)
        cur = jnp.where(cur == mx, -(j + 1) * RANK_BASE, cur)
    rank = jnp.where(cur < -0.5 * RANK_BASE, cur * (-1.0 / RANK_BASE) - 1.0, float(k))
    return rows, rank


def _route_kernel(xb_ref, wq_ref, keys_ref, cnt_ref, p1_ref, r2_ref, p2_ref, s_scr, *, heads, topk):
    nk = keys_ref.shape[1]

    q_t = _dot_nt(wq_ref[...], xb_ref[...])
    for hp in range(2 * heads):
        s_scr[hp] = _dot(keys_ref[hp], q_t[hp * nk:(hp + 1) * nk, :].astype(BF16))

    def head(h, carry):
        s1 = s_scr[2 * h]
        s2 = s_scr[2 * h + 1]
        a, rank1 = _top_sorted(s1, topk)
        b, rank2 = _top_sorted(s2, topk)
        a_all = jnp.concatenate(a, axis=0)
        b_all = jnp.concatenate(b, axis=0)
        half_rows = topk // 2
        j_idx = lax.broadcasted_iota(jnp.int32, (half_rows, 1), 0)
        cands = [a_all + b[0]]
        for l in range(1, half_rows):
            cands.append(jnp.where(j_idx < topk // (l + 1), a_all[:half_rows] + b[l], -jnp.inf))
        cands.append(a[0] + b_all[half_rows:])
        cur = jnp.concatenate(cands, axis=0)
        m = a[0] + b[0]
        z = jnp.zeros_like(m)
        tau = m
        for r in range(topk):
            tau = jnp.max(cur, axis=0, keepdims=True)
            z = z + jnp.exp(tau - m)
            if r + 1 < topk:
                cur = jnp.where(cur == tau, -jnp.inf, cur)
        hit = lambda c: jnp.where(c >= tau, 1.0, 0.0)
        cnt_lo = hit(cands[0][:half_rows])
        for l in range(1, half_rows):
            cnt_lo = cnt_lo + hit(cands[l])
        cnt_0 = cnt_lo[0:1] + jnp.sum(hit(cands[half_rows]), axis=0, keepdims=True)
        cnt_rows = jnp.concatenate([cnt_lo, hit(cands[0][half_rows:])], axis=0)
        cnt = jnp.zeros(s1.shape, F32)
        for c in range(half_rows):
            ranks_with = jnp.sum(jnp.where(cnt_rows > float(c), 1.0, 0.0), axis=0, keepdims=True)
            cnt = jnp.where(rank1 < ranks_with, float(c + 1), cnt)
        cnt = jnp.where(rank1 == 0.0, jnp.maximum(cnt, cnt_0), cnt)
        cnt_ref[h] = cnt.astype(BF16)
        p1_ref[h] = jnp.exp(s1 - a[0]).astype(BF16)
        r2_ref[h] = rank2.astype(BF16)
        p2_ref[h] = (jnp.exp(s2 - b[0]) / z).astype(BF16)
        return carry

    lax.fori_loop(0, heads, head, 0)


def _route(x1b, wq_t, keys, tt):
    n, d = x1b.shape
    hp, nk, _ = keys.shape
    heads = hp // 2
    kern = functools.partial(_route_kernel, heads=heads, topk=PEER_TOPK)
    out = lambda dt: jax.ShapeDtypeStruct((heads, nk, n), dt)
    ospec = pl.BlockSpec((heads, nk, tt), lambda i: (0, 0, i))
    return pl.pallas_call(
        kern,
        grid=(n // tt,),
        in_specs=[pl.BlockSpec((tt, d), lambda i: (i, 0)),
                  pl.BlockSpec(wq_t.shape, lambda i: (0, 0)),
                  pl.BlockSpec(keys.shape, lambda i: (0, 0, 0))],
        out_specs=[ospec, ospec, ospec, ospec],
        out_shape=[out(BF16), out(BF16), out(BF16), out(BF16)],
        scratch_shapes=[pltpu.VMEM((hp, nk, tt), F32)],
        compiler_params=_cparams(("parallel",)),
        name="route",
    )(x1b, wq_t, keys)


def _peer_kernel(xb_ref, x_ref, u_ref, vt_ref, cnt_ref, p1_ref, r2_ref, p2_ref, g_ref, b_ref,
                 out_ref, acc_ref, h_ref, a_ref, *, heads, nk, alpha):
    j = pl.program_id(1)
    tt = xb_ref.shape[0]
    n_i1 = u_ref.shape[0] // nk
    pk = BF16_SUBLANES
    tiles = nk // pk

    @pl.when(j == 0)
    def _():
        acc_ref[...] = jnp.zeros_like(acc_ref)

    h_ref[...] = _dot_nt(u_ref[...], xb_ref[...]).astype(BF16).reshape(n_i1 * tiles, pk, tt)

    i1_rows = pl.ds(pl.multiple_of(j * n_i1, n_i1), n_i1)
    cnt_t = [cnt_ref[h, i1_rows, :].astype(F32) for h in range(heads)]
    p1_t = [p1_ref[h, i1_rows, :].astype(F32) for h in range(heads)]
    for ii in range(n_i1):
        w = jnp.zeros((tiles, pk, tt), BF16)
        for h in range(heads):
            cnt = jnp.broadcast_to(cnt_t[h][ii:ii + 1], (pk, tt)).astype(BF16)
            p1 = jnp.broadcast_to(p1_t[h][ii:ii + 1], (pk, tt)).astype(BF16)
            r2 = r2_ref[h].reshape(tiles, pk, tt)
            p2 = p2_ref[h].reshape(tiles, pk, tt)
            w = w + jnp.where(r2 < cnt[None], p2, 0.0) * p1[None]
        a_ref[ii * tiles:(ii + 1) * tiles] = w * _gelu_tanh(h_ref[ii * tiles:(ii + 1) * tiles])
    acc_ref[...] += _dot(vt_ref[...], a_ref[...].reshape(n_i1 * nk, tt))

    @pl.when(j == pl.num_programs(1) - 1)
    def _():
        z = alpha * x_ref[...] + acc_ref[...].T
        out_ref[...] = _layer_norm_rows(z, g_ref[...], b_ref[...])


def _peer(x1, x1b, u_b, vt_b, routing, g_row, b_row, alpha, tt, chunk):
    n, d = x1.shape
    cnt, p1, r2, p2 = routing
    heads, nk, _ = cnt.shape
    n_exp = u_b.shape[0]
    assert chunk == BF16_SUBLANES * nk, "one grid step covers one packed sublane tile of i1 values"
    kern = functools.partial(_peer_kernel, heads=heads, nk=nk, alpha=alpha)
    rspec = pl.BlockSpec((heads, nk, tt), lambda i, j: (0, 0, i))
    packed = (chunk // BF16_SUBLANES, BF16_SUBLANES, tt)
    return pl.pallas_call(
        kern,
        grid=(n // tt, n_exp // chunk),
        in_specs=[pl.BlockSpec((tt, d), lambda i, j: (i, 0)),
                  pl.BlockSpec((tt, d), lambda i, j: (i, 0)),
                  pl.BlockSpec((chunk, d), lambda i, j: (j, 0)),
                  pl.BlockSpec((d, chunk), lambda i, j: (0, j)),
                  rspec, rspec, rspec, rspec,
                  pl.BlockSpec((1, d), lambda i, j: (0, 0)),
                  pl.BlockSpec((1, d), lambda i, j: (0, 0))],
        out_specs=pl.BlockSpec((tt, d), lambda i, j: (i, 0)),
        out_shape=jax.ShapeDtypeStruct((n, d), F32),
        scratch_shapes=[pltpu.VMEM((d, tt), F32), pltpu.VMEM(packed, BF16), pltpu.VMEM(packed, BF16)],
        compiler_params=_cparams(("parallel", "arbitrary")),
        name="peer",
    )(x1b, x1, u_b, vt_b, cnt, p1, r2, p2, g_row, b_row)


def _stage2(x1, x1b, w_pq, sub_keys, peer_u, peer_v, ln2_g, ln2_b, alpha):
    n, d = x1.shape
    heads, _, nk, half = sub_keys.shape
    tt = min(512, n)
    routing = _route(x1b, w_pq.T.astype(BF16), sub_keys.reshape(heads * 2, nk, half).astype(BF16), tt)
    return _peer(x1, x1b, peer_u.astype(BF16), peer_v.T.astype(BF16), routing,
                 ln2_g.astype(F32).reshape(1, d), ln2_b.astype(F32).reshape(1, d), alpha, tt, chunk=BF16_SUBLANES * nk)


def kernel(x, w_in, s5_lam_re, s5_lam_im, s5_b_re, s5_b_im, s5_c_re, s5_c_im, s5_d, s5_log_step, s5_w_glu,
           s5_b_glu, hgrn_lb_logits, hgrn_gnorm_w, w_pa, w_pb, w_o, ln1_g, ln1_b, peer_w_q, peer_sub_keys,
           peer_u, peer_v, ln2_g, ln2_b):
    depth = w_in.shape[0]
    alpha = (2.0 * depth) ** 0.25
    lb_all = jnp.cumsum(jax.nn.softmax(hgrn_lb_logits.astype(F32), axis=0), axis=0)[:depth]
    bsz, seq, d = x.shape
    for l in range(depth):
        x1, x1b = _stage1(x, w_in[l], s5_lam_re[l], s5_lam_im[l], s5_b_re[l], s5_b_im[l], s5_c_re[l], s5_c_im[l],
                          s5_d[l], s5_log_step[l], s5_w_glu[l], s5_b_glu[l], lb_all[l], hgrn_gnorm_w[l],
                          w_pa[l], w_pb[l], w_o[l], ln1_g[l], ln1_b[l], alpha)
        x = _stage2(x1, x1b, peer_w_q[l], peer_sub_keys[l], peer_u[l], peer_v[l], ln2_g[l], ln2_b[l],
                    alpha).reshape(bsz, seq, d)
    return x
```

```python
import functools
import math

import numpy as np
import jax
import jax.numpy as jnp
from jax import lax
from jax.experimental import pallas as pl
from jax.experimental.pallas import tpu as pltpu

F32 = jnp.float32
BF16 = jnp.bfloat16

V7X_VMEM_LIMIT_BYTES = 56 * 1024 * 1024
LANES = 128
SUBLANES = 8
BF16_SUBLANES = 16

S5_GROUP = 16
S5_STATE = 64
S5_GROUPS_PER_BLOCK = 8
HGRN_HEAD_DIM = 128
HGRN_CHUNK = 128
PEER_HEADS = 8
PEER_TOPK = 16
PEER_NKEYS = 128
LN_EPS = 1e-5
RMS_EPS = 1e-6


def _cparams(semantics):
    return pltpu.CompilerParams(dimension_semantics=semantics, vmem_limit_bytes=V7X_VMEM_LIMIT_BYTES)


def _dot(a, b):
    return jnp.dot(a, b, preferred_element_type=F32)


def _dot_nt(a, b):
    return lax.dot_general(a, b, (((1,), (1,)), ((), ())), preferred_element_type=F32)


def _dot_tn(a, b):
    return lax.dot_general(a, b, (((0,), (0,)), ((), ())), preferred_element_type=F32)


def _sigmoid(x):
    return 1.0 / (1.0 + jnp.exp(-x))


def _gelu_tanh(x):
    c = math.sqrt(2.0 / math.pi)
    return 0.5 * x * (1.0 + jnp.tanh(c * (x + 0.044715 * (x * x * x))))


def _layer_norm_rows(z, g, b):
    mu = jnp.mean(z, axis=-1, keepdims=True)
    zc = z - mu
    var = jnp.mean(zc * zc, axis=-1, keepdims=True)
    return zc * lax.rsqrt(var + LN_EPS) * g + b


def _proj_kernel(x_ref, w_ref, u_ref, hg_ref, gate_ref, *, s5_w, hg_w, col_chunk):
    xb = x_ref[...].astype(BF16)
    n_cols = w_ref.shape[1]
    for c0 in range(0, n_cols, col_chunk):
        p = _dot(xb, w_ref[:, c0:c0 + col_chunk]).astype(BF16)
        if c0 < s5_w:
            u_ref[:, c0:c0 + col_chunk] = p
        elif c0 < s5_w + hg_w:
            hg_ref[:, c0 - s5_w:c0 - s5_w + col_chunk] = p
        else:
            o = c0 - s5_w - hg_w
            gate_ref[:, o:o + col_chunk] = p


def _proj(x2, w_in_b, bsz, seq, s5_w, hg_w, gate_w, tm):
    n, d = x2.shape
    n_s = seq // tm
    kern = functools.partial(_proj_kernel, s5_w=s5_w, hg_w=hg_w, col_chunk=s5_w)
    return pl.pallas_call(
        kern,
        grid=(bsz, n_s),
        in_specs=[
            pl.BlockSpec((tm, d), lambda b, i: (b * n_s + i, 0)),
            pl.BlockSpec(w_in_b.shape, lambda b, i: (0, 0)),
        ],
        out_specs=[
            pl.BlockSpec((tm, s5_w), lambda b, i: (i, b)),
            pl.BlockSpec((tm, hg_w), lambda b, i: (b * n_s + i, 0)),
            pl.BlockSpec((tm, gate_w), lambda b, i: (b * n_s + i, 0)),
        ],
        out_shape=[
            jax.ShapeDtypeStruct((seq, bsz * s5_w), BF16),
            jax.ShapeDtypeStruct((n, hg_w), BF16),
            jax.ShapeDtypeStruct((n, gate_w), BF16),
        ],
        compiler_params=_cparams(("parallel", "parallel")),
        name="proj",
    )(x2, w_in_b)


def _s5_kernel(u_ref, bblk_ref, cblk_ref, lamr_ref, lami_ref, d_ref, wglu_ref, bglu_ref,
               y_ref, xs_ref, st_ref, *, bsz, tb, n_blk, half):
    @pl.when(pl.program_id(0) == 0)
    def _():
        st_ref[...] = jnp.zeros_like(st_ref)

    in_w = bblk_ref.shape[1]
    ys = []
    for j in range(n_blk):
        uj = u_ref[:, j * in_w:(j + 1) * in_w]
        xs_ref[...] = _dot(uj, bblk_ref[j])
        lr = jnp.broadcast_to(lamr_ref[j], (bsz, half))
        li = jnp.broadcast_to(lami_ref[j], (bsz, half))

        def body(t, carry, lr=lr, li=li):
            sr, si = carry
            rows = pl.ds(pl.multiple_of(t * bsz, bsz), bsz)
            nr = lr * sr - li * si + xs_ref[rows, :half]
            ni = lr * si + li * sr + xs_ref[rows, half:]
            xs_ref[rows, :half] = nr
            xs_ref[rows, half:] = ni
            return nr, ni

        sr, si = lax.fori_loop(0, tb, body, (st_ref[j, :, :half], st_ref[j, :, half:]), unroll=True)
        st_ref[j, :, :half] = sr
        st_ref[j, :, half:] = si
        ys.append(_dot(xs_ref[...].astype(BF16), cblk_ref[j]))
    y = jnp.concatenate(ys, axis=1) + d_ref[...] * u_ref[...].astype(F32)
    y = _gelu_tanh(y)
    z = _dot(y.astype(BF16), wglu_ref[...]) + bglu_ref[...]
    y_ref[...] = (y * _sigmoid(z)).astype(BF16)


def _s5_params(lam_re, lam_im, b_re, b_im, c_re, c_im, d_skip, log_step):
    g, p = lam_re.shape
    h = b_re.shape[-1]
    gb = S5_GROUPS_PER_BLOCK
    nb = g // gb
    lr, li = lam_re.astype(F32), lam_im.astype(F32)
    dt = jnp.exp(log_step.astype(F32))[:, None]
    mag = jnp.exp(lr * dt)
    lbr, lbi = mag * jnp.cos(li * dt), mag * jnp.sin(li * dt)
    den = lr * lr + li * li
    cr = ((lbr - 1.0) * lr + lbi * li) / den
    ci = (lbi * lr - (lbr - 1.0) * li) / den
    br, bi = b_re.astype(F32), b_im.astype(F32)
    bbar_r = cr[..., None] * br - ci[..., None] * bi
    bbar_i = cr[..., None] * bi + ci[..., None] * br
    eye = jnp.eye(gb, dtype=F32)

    def in_blk(m):
        m = m.reshape(nb, gb, p, h)
        return jnp.einsum('jgph,gk->jghkp', m, eye).reshape(nb, gb * h, gb * p)

    def out_blk(m):
        m = m.reshape(nb, gb, h, p)
        return jnp.einsum('jghp,gk->jgpkh', m, eye).reshape(nb, gb * p, gb * h)

    bblk = jnp.concatenate([in_blk(bbar_r), in_blk(bbar_i)], axis=2).astype(BF16)
    cblk = jnp.concatenate([out_blk(c_re.astype(F32)), out_blk(-c_im.astype(F32))], axis=1).astype(BF16)
    lamr = lbr.reshape(nb, 1, gb * p)
    lami = lbi.reshape(nb, 1, gb * p)
    return bblk, cblk, lamr, lami, d_skip.astype(F32).reshape(1, g * h)


def _s5(u_tm, params, w_glu_b, b_glu, bsz, seq, tb):
    bblk, cblk, lamr, lami, d_row = params
    n, w = u_tm.shape
    n_blk, in_w, two_half = bblk.shape
    half = two_half // 2
    rows = tb * bsz
    kern = functools.partial(_s5_kernel, bsz=bsz, tb=tb, n_blk=n_blk, half=half)
    const = lambda a: pl.BlockSpec(a.shape, lambda i: (0,) * a.ndim)
    return pl.pallas_call(
        kern,
        grid=(seq // tb,),
        in_specs=[pl.BlockSpec((rows, w), lambda i: (i, 0)),
                  const(bblk), const(cblk), const(lamr), const(lami), const(d_row),
                  const(w_glu_b), const(b_glu)],
        out_specs=pl.BlockSpec((rows, w), lambda i: (i, 0)),
        out_shape=jax.ShapeDtypeStruct((n, w), BF16),
        scratch_shapes=[pltpu.VMEM((rows, two_half), F32), pltpu.VMEM((n_blk, bsz, two_half), F32)],
        compiler_params=_cparams(("arbitrary",)),
        name="s5",
    )(u_tm, bblk, cblk, lamr, lami, d_row, w_glu_b, b_glu)


def _hgrn_tables(c):
    n_lev = int(math.log2(c))
    t = np.arange(c)[:, None]
    r = np.arange(c)[None, :]
    sel = []
    masks = []
    for lev in range(n_lev):
        w = c >> (lev + 1)
        pos = t % (2 * w)
        a = t - pos + w - 1
        upper = pos >= w
        sel.append(np.where(upper, (r > a) & (r <= t), (r > t) & (r <= a)))
        tt, ss = np.arange(c)[:, None], np.arange(c)[None, :]
        masks.append((tt // (2 * w) == ss // (2 * w)) & (tt % (2 * w) >= w) & (ss % (2 * w) < w))
    sel.append(r <= t)
    sel.append(r > t)
    masks.append(np.eye(c, dtype=bool))
    sel = np.concatenate(sel, axis=0).astype(np.float32)
    sel = np.concatenate([sel, sel], axis=1)
    masks = np.stack(masks, axis=0).astype(np.float32)
    return jnp.asarray(sel, BF16), jnp.asarray(masks, F32)


def _hgrn_kernel(hg_ref, sel_ref, mask_ref, lb_ref, gw_ref, y_ref, st_ref, *, c, n_chunks, width, heads):
    @pl.when(pl.program_id(1) == 0)
    def _():
        st_ref[...] = jnp.zeros_like(st_ref)

    n_lev = mask_ref.shape[0] - 1
    dh = width // heads
    lb = lb_ref[...]
    gw = gw_ref[...]

    def chunk(ci, carry):
        rows = pl.ds(pl.multiple_of(ci * c, c), c)
        q = hg_ref[0, rows, 0:width].astype(F32)
        fp = hg_ref[0, rows, width:2 * width].astype(F32)
        v = hg_ref[0, rows, 2 * width:3 * width]
        g = hg_ref[0, rows, 3 * width:4 * width].astype(F32)
        f = lb + (1.0 - lb) * _sigmoid(fp)
        lf = jnp.log(f)
        k = 1.0 - f
        qf = q * _sigmoid(q)
        h1 = lf.astype(BF16)
        h2 = (lf - h1.astype(F32)).astype(BF16)
        e_all = jnp.exp(_dot(sel_ref[...], jnp.concatenate([h1, h2], axis=0)))
        e_cum = e_all[n_lev * c:(n_lev + 1) * c]
        e_suf = e_all[(n_lev + 1) * c:(n_lev + 2) * c]
        outs = []
        for hd in range(heads):
            cols = slice(hd * dh, (hd + 1) * dh)
            qh, kh = qf[:, cols], k[:, cols]
            vh = v[:, cols]
            sc = mask_ref[n_lev] * _dot_nt(qh.astype(BF16), kh.astype(BF16))
            for lev in range(n_lev):
                el = e_all[lev * c:(lev + 1) * c, cols]
                sc = sc + mask_ref[lev] * _dot_nt((qh * el).astype(BF16), (kh * el).astype(BF16))
            st = st_ref[hd]
            o = _dot_nt((qh * e_cum[:, cols]).astype(BF16), st.astype(BF16)) + _dot(sc.astype(BF16), vh)
            e_tot = e_cum[c - 1:c, cols]
            st_ref[hd] = st * e_tot + _dot_tn(vh, (kh * e_suf[:, cols]).astype(BF16))
            o = o * lax.rsqrt(jnp.mean(o * o, axis=-1, keepdims=True) + RMS_EPS)
            outs.append(o)
        o = jnp.concatenate(outs, axis=1) * gw * (g * _sigmoid(g))
        y_ref[0, rows, :] = o.astype(BF16)
        return carry

    lax.fori_loop(0, n_chunks, chunk, 0, unroll=2)


def _hgrn(hg3, lb_row, gw_row, heads, tbh):
    bsz, seq, w4 = hg3.shape
    width = w4 // 4
    c = HGRN_CHUNK
    sel, masks = _hgrn_tables(c)
    kern = functools.partial(_hgrn_kernel, c=c, n_chunks=tbh // c, width=width, heads=heads)
    const = lambda a: pl.BlockSpec(a.shape, lambda b, i: (0,) * a.ndim)
    return pl.pallas_call(
        kern,
        grid=(bsz, seq // tbh),
        in_specs=[pl.BlockSpec((1, tbh, w4), lambda b, i: (b, i, 0)),
                  const(sel), const(masks), const(lb_row), const(gw_row)],
        out_specs=pl.BlockSpec((1, tbh, width), lambda b, i: (b, i, 0)),
        out_shape=jax.ShapeDtypeStruct((bsz, seq, width), BF16),
        scratch_shapes=[pltpu.VMEM((heads, width // heads, width // heads), F32)],
        compiler_params=_cparams(("parallel", "arbitrary")),
        name="hgrn",
    )(hg3, sel, masks, lb_row, gw_row)


def _merge_kernel(x_ref, ya_ref, yb_ref, gate_ref, wpa_ref, wpb_ref, wo_ref, g_ref, b_ref,
                  x1_ref, x1b_ref, *, alpha, d):
    ga = gate_ref[:, :d].astype(F32)
    gb = gate_ref[:, d:].astype(F32)
    merged = _sigmoid(ga) * _dot(ya_ref[...], wpa_ref[...]) + _sigmoid(gb) * _dot(yb_ref[...], wpb_ref[...])
    mix = _dot(merged.astype(BF16), wo_ref[...])
    x1 = _layer_norm_rows(alpha * x_ref[...] + mix, g_ref[...], b_ref[...])
    x1_ref[...] = x1
    x1b_ref[...] = x1.astype(BF16)


def _merge(x2, ya_tm, yb2, gates, wpa_b, wpb_b, wo_b, g_row, b_row, bsz, seq, alpha, tm):
    n, d = x2.shape
    w = yb2.shape[1]
    n_s = seq // tm
    kern = functools.partial(_merge_kernel, alpha=alpha, d=d)
    const = lambda a: pl.BlockSpec(a.shape, lambda b, i: (0,) * a.ndim)
    row = lambda cols: pl.BlockSpec((tm, cols), lambda b, i: (b * n_s + i, 0))
    return pl.pallas_call(
        kern,
        grid=(bsz, n_s),
        in_specs=[row(d),
                  pl.BlockSpec((tm, w), lambda b, i: (i, b)),
                  row(w), row(2 * d),
                  const(wpa_b), const(wpb_b), const(wo_b), const(g_row), const(b_row)],
        out_specs=[row(d), row(d)],
        out_shape=[jax.ShapeDtypeStruct((n, d), F32), jax.ShapeDtypeStruct((n, d), BF16)],
        compiler_params=_cparams(("parallel", "parallel")),
        name="merge",
    )(x2, ya_tm, yb2, gates, wpa_b, wpb_b, wo_b, g_row, b_row)


def _stage1(x, w_in, s5_lam_re, s5_lam_im, s5_b_re, s5_b_im, s5_c_re, s5_c_im, s5_d, s5_log_step,
            s5_w_glu, s5_b_glu, lb, gnorm_w, w_pa, w_pb, w_o, ln1_g, ln1_b, alpha):
    bsz, seq, d = x.shape
    n = bsz * seq
    s5_w = s5_w_glu.shape[0]
    hg_w = 4 * gnorm_w.shape[0]
    gate_w = 2 * d
    tm = min(512, seq)
    x2 = x.reshape(n, d)
    u_tm, hg, gates = _proj(x2, w_in.astype(BF16), bsz, seq, s5_w, hg_w, gate_w, tm)
    s5p = _s5_params(s5_lam_re, s5_lam_im, s5_b_re, s5_b_im, s5_c_re, s5_c_im, s5_d, s5_log_step)
    ya = _s5(u_tm.reshape(seq * bsz, s5_w), s5p, s5_w_glu.astype(BF16),
             s5_b_glu.astype(F32).reshape(1, s5_w), bsz, seq, tb=min(128, seq))
    yb = _hgrn(hg.reshape(bsz, seq, hg_w), lb.reshape(1, -1), gnorm_w.astype(F32).reshape(1, -1),
               heads=gnorm_w.shape[0] // HGRN_HEAD_DIM, tbh=min(512, seq))
    return _merge(x2, ya.reshape(seq, bsz * s5_w), yb.reshape(n, -1), gates,
                  w_pa.astype(BF16), w_pb.astype(BF16), w_o.astype(BF16),
                  ln1_g.astype(F32).reshape(1, d), ln1_b.astype(F32).reshape(1, d), bsz, seq, alpha, tm)


RANK_BASE = 2.0 ** 100


def _top_sorted(s, k):
    rows = []
    cur = s
    for j in range(k):
        mx = jnp.max(cur, axis=0, keepdims=True)
        rows.append(mx)
        cur = jnp.where(cur == mx, -(j + 1) * RANK_BASE, cur)
    rank = jnp.where(cur < -0.5 * RANK_BASE, cur * (-1.0 / RANK_BASE) - 1.0, float(k))
    return rows, rank


def _route_kernel(xb_ref, wq_ref, keys_ref, cnt_ref, p1_ref, r2_ref, p2_ref, s_scr, *, heads, topk):
    nk = keys_ref.shape[1]

    q_t = _dot_nt(wq_ref[...], xb_ref[...])
    for hp in range(2 * heads):
        s_scr[hp] = _dot(keys_ref[hp], q_t[hp * nk:(hp + 1) * nk, :].astype(BF16))

    def head(h, carry):
        s1 = s_scr[2 * h]
        s2 = s_scr[2 * h + 1]
        a, rank1 = _top_sorted(s1, topk)
        b, rank2 = _top_sorted(s2, topk)
        a_all = jnp.concatenate(a, axis=0)
        b_all = jnp.concatenate(b, axis=0)
        half_rows = topk // 2
        j_idx = lax.broadcasted_iota(jnp.int32, (half_rows, 1), 0)
        cands = [a_all + b[0]]
        for l in range(1, half_rows):
            cands.append(jnp.where(j_idx < topk // (l + 1), a_all[:half_rows] + b[l], -jnp.inf))
        cands.append(a[0] + b_all[half_rows:])
        cur = jnp.concatenate(cands, axis=0)
        m = a[0] + b[0]
        z = jnp.zeros_like(m)
        tau = m
        for r in range(topk):
            tau = jnp.max(cur, axis=0, keepdims=True)
            z = z + jnp.exp(tau - m)
            if r + 1 < topk:
                cur = jnp.where(cur == tau, -jnp.inf, cur)
        hit = lambda c: jnp.where(c >= tau, 1.0, 0.0)
        cnt_lo = hit(cands[0][:half_rows])
        for l in range(1, half_rows):
            cnt_lo = cnt_lo + hit(cands[l])
        cnt_0 = cnt_lo[0:1] + jnp.sum(hit(cands[half_rows]), axis=0, keepdims=True)
        cnt_rows = jnp.concatenate([cnt_lo, hit(cands[0][half_rows:])], axis=0)
        cnt = jnp.zeros(s1.shape, F32)
        for c in range(half_rows):
            ranks_with = jnp.sum(jnp.where(cnt_rows > float(c), 1.0, 0.0), axis=0, keepdims=True)
            cnt = jnp.where(rank1 < ranks_with, float(c + 1), cnt)
        cnt = jnp.where(rank1 == 0.0, jnp.maximum(cnt, cnt_0), cnt)
        cnt_ref[h] = cnt.astype(BF16)
        p1_ref[h] = jnp.exp(s1 - a[0]).astype(BF16)
        r2_ref[h] = rank2.astype(BF16)
        p2_ref[h] = (jnp.exp(s2 - b[0]) / z).astype(BF16)
        return carry

    lax.fori_loop(0, heads, head, 0)


def _route(x1b, wq_t, keys, tt):
    n, d = x1b.shape
    hp, nk, _ = keys.shape
    heads = hp // 2
    kern = functools.partial(_route_kernel, heads=heads, topk=PEER_TOPK)
    out = lambda dt: jax.ShapeDtypeStruct((heads, nk, n), dt)
    ospec = pl.BlockSpec((heads, nk, tt), lambda i: (0, 0, i))
    return pl.pallas_call(
        kern,
        grid=(n // tt,),
        in_specs=[pl.BlockSpec((tt, d), lambda i: (i, 0)),
                  pl.BlockSpec(wq_t.shape, lambda i: (0, 0)),
                  pl.BlockSpec(keys.shape, lambda i: (0, 0, 0))],
        out_specs=[ospec, ospec, ospec, ospec],
        out_shape=[out(BF16), out(BF16), out(BF16), out(BF16)],
        scratch_shapes=[pltpu.VMEM((hp, nk, tt), F32)],
        compiler_params=_cparams(("parallel",)),
        name="route",
    )(x1b, wq_t, keys)


PEER_DOT_SPLITS = (2, 6, 8)


def _peer_kernel(xb_ref, x_ref, u_ref, vt_ref, cnt_ref, p1_ref, r2_ref, p2_ref, g_ref, b_ref,
                 out_ref, acc_ref, h_ref, a_ref, *, heads, nk, alpha):
    j = pl.program_id(1)
    tt = xb_ref.shape[0]
    n_i1 = u_ref.shape[0] // nk
    pk = BF16_SUBLANES
    tiles = nk // pk

    @pl.when(j == 0)
    def _():
        acc_ref[...] = jnp.zeros_like(acc_ref)

    i1_rows = pl.ds(pl.multiple_of(j * n_i1, n_i1), n_i1)
    cnt_t = [cnt_ref[h, i1_rows, :].astype(F32) for h in range(heads)]
    p1_t = [p1_ref[h, i1_rows, :].astype(F32) for h in range(heads)]

    def expert_dots(i0, n):
        h = _dot_nt(u_ref[i0 * nk:(i0 + n) * nk, :], xb_ref[...])
        h_ref[i0 * tiles:(i0 + n) * tiles] = h.astype(BF16).reshape(n * tiles, pk, tt)

    def gate_slab(ii):
        w = jnp.zeros((tiles, pk, tt), BF16)
        for h in range(heads):
            cnt = jnp.broadcast_to(cnt_t[h][ii:ii + 1], (pk, tt)).astype(BF16)
            p1 = jnp.broadcast_to(p1_t[h][ii:ii + 1], (pk, tt)).astype(BF16)
            r2 = r2_ref[h].reshape(tiles, pk, tt)
            p2 = p2_ref[h].reshape(tiles, pk, tt)
            w = w + jnp.where(r2 < cnt[None], p2, 0.0) * p1[None]
        a_ref[ii * tiles:(ii + 1) * tiles] = w * _gelu_tanh(h_ref[ii * tiles:(ii + 1) * tiles])

    starts = [sum(PEER_DOT_SPLITS[:k]) for k in range(len(PEER_DOT_SPLITS))]
    expert_dots(0, PEER_DOT_SPLITS[0])
    for k, n in enumerate(PEER_DOT_SPLITS):
        if k + 1 < len(PEER_DOT_SPLITS):
            expert_dots(starts[k + 1], PEER_DOT_SPLITS[k + 1])
        for ii in range(starts[k], starts[k] + n):
            gate_slab(ii)
    acc_ref[...] += _dot(vt_ref[...], a_ref[...].reshape(n_i1 * nk, tt))

    @pl.when(j == pl.num_programs(1) - 1)
    def _():
        z = alpha * x_ref[...] + acc_ref[...].T
        out_ref[...] = _layer_norm_rows(z, g_ref[...], b_ref[...])


def _peer(x1, x1b, u_b, vt_b, routing, g_row, b_row, alpha, tt, chunk):
    n, d = x1.shape
    cnt, p1, r2, p2 = routing
    heads, nk, _ = cnt.shape
    n_exp = u_b.shape[0]
    assert chunk == BF16_SUBLANES * nk, "one grid step covers one packed sublane tile of i1 values"
    assert sum(PEER_DOT_SPLITS) == BF16_SUBLANES
    kern = functools.partial(_peer_kernel, heads=heads, nk=nk, alpha=alpha)
    rspec = pl.BlockSpec((heads, nk, tt), lambda i, j: (0, 0, i))
    packed = (chunk // BF16_SUBLANES, BF16_SUBLANES, tt)
    return pl.pallas_call(
        kern,
        grid=(n // tt, n_exp // chunk),
        in_specs=[pl.BlockSpec((tt, d), lambda i, j: (i, 0)),
                  pl.BlockSpec((tt, d), lambda i, j: (i, 0)),
                  pl.BlockSpec((chunk, d), lambda i, j: (j, 0)),
                  pl.BlockSpec((d, chunk), lambda i, j: (0, j)),
                  rspec, rspec, rspec, rspec,
                  pl.BlockSpec((1, d), lambda i, j: (0, 0)),
                  pl.BlockSpec((1, d), lambda i, j: (0, 0))],
        out_specs=pl.BlockSpec((tt, d), lambda i, j: (i, 0)),
        out_shape=jax.ShapeDtypeStruct((n, d), F32),
        scratch_shapes=[pltpu.VMEM((d, tt), F32), pltpu.VMEM(packed, BF16), pltpu.VMEM(packed, BF16)],
        compiler_params=_cparams(("parallel", "arbitrary")),
        name="peer",
    )(x1b, x1, u_b, vt_b, cnt, p1, r2, p2, g_row, b_row)


def _stage2(x1, x1b, w_pq, sub_keys, peer_u, peer_v, ln2_g, ln2_b, alpha):
    n, d = x1.shape
    heads, _, nk, half = sub_keys.shape
    tt = min(512, n)
    routing = _route(x1b, w_pq.T.astype(BF16), sub_keys.reshape(heads * 2, nk, half).astype(BF16), tt)
    return _peer(x1, x1b, peer_u.astype(BF16), peer_v.T.astype(BF16), routing,
                 ln2_g.astype(F32).reshape(1, d), ln2_b.astype(F32).reshape(1, d), alpha, tt, chunk=BF16_SUBLANES * nk)


def kernel(x, w_in, s5_lam_re, s5_lam_im, s5_b_re, s5_b_im, s5_c_re, s5_c_im, s5_d, s5_log_step, s5_w_glu,
           s5_b_glu, hgrn_lb_logits, hgrn_gnorm_w, w_pa, w_pb, w_o, ln1_g, ln1_b, peer_w_q, peer_sub_keys,
           peer_u, peer_v, ln2_g, ln2_b):
    depth = w_in.shape[0]
    alpha = (2.0 * depth) ** 0.25
    lb_all = jnp.cumsum(jax.nn.softmax(hgrn_lb_logits.astype(F32), axis=0), axis=0)[:depth]
    bsz, seq, d = x.shape
    for l in range(depth):
        x1, x1b = _stage1(x, w_in[l], s5_lam_re[l], s5_lam_im[l], s5_b_re[l], s5_b_im[l], s5_c_re[l], s5_c_im[l],
                          s5_d[l], s5_log_step[l], s5_w_glu[l], s5_b_glu[l], lb_all[l], hgrn_gnorm_w[l],
                          w_pa[l], w_pb[l], w_o[l], ln1_g[l], ln1_b[l], alpha)
        x = _stage2(x1, x1b, peer_w_q[l], peer_sub_keys[l], peer_u[l], peer_v[l], ln2_g[l], ln2_b[l],
                    alpha).reshape(bsz, seq, d)
    return x
```

```python
import functools
import math

import numpy as np
import jax
import jax.numpy as jnp
from jax import lax
from jax.experimental import pallas as pl
from jax.experimental.pallas import tpu as pltpu

F32 = jnp.float32
BF16 = jnp.bfloat16

V7X_VMEM_LIMIT_BYTES = 58 * 1024 * 1024
LANES = 128
BF16_SUBLANES = 16

S5_GROUPS_PER_BLOCK = 8
HGRN_HEAD_DIM = 128
HGRN_CHUNK = 128
PEER_TOPK = 16
LN_EPS = 1e-5
RMS_EPS = 1e-6


def _cparams(semantics):
    return pltpu.CompilerParams(dimension_semantics=semantics, vmem_limit_bytes=V7X_VMEM_LIMIT_BYTES)


def _dot(a, b):
    return jnp.dot(a, b, preferred_element_type=F32)


def _dot_nt(a, b):
    return lax.dot_general(a, b, (((1,), (1,)), ((), ())), preferred_element_type=F32)


def _dot_tn(a, b):
    return lax.dot_general(a, b, (((0,), (0,)), ((), ())), preferred_element_type=F32)


def _sigmoid(x):
    return 1.0 / (1.0 + jnp.exp(-x))


def _gelu_tanh(x):
    c = math.sqrt(2.0 / math.pi)
    return 0.5 * x * (1.0 + jnp.tanh(c * (x + 0.044715 * (x * x * x))))


def _layer_norm_rows(z, g, b):
    mu = jnp.mean(z, axis=-1, keepdims=True)
    zc = z - mu
    var = jnp.mean(zc * zc, axis=-1, keepdims=True)
    return zc * lax.rsqrt(var + LN_EPS) * g + b


def _proj_kernel(x_ref, w_ref, u_ref, hg_ref, gate_ref, *, s5_w, hg_w, col_chunk):
    xb = x_ref[...].astype(BF16)
    n_cols = w_ref.shape[1]
    for c0 in range(0, n_cols, col_chunk):
        p = _dot(xb, w_ref[:, c0:c0 + col_chunk]).astype(BF16)
        if c0 < s5_w:
            u_ref[:, c0:c0 + col_chunk] = p
        elif c0 < s5_w + hg_w:
            hg_ref[:, c0 - s5_w:c0 - s5_w + col_chunk] = p
        else:
            o = c0 - s5_w - hg_w
            gate_ref[:, o:o + col_chunk] = p


def _proj(x2, w_in_b, bsz, seq, s5_w, hg_w, gate_w, tm):
    n, d = x2.shape
    n_s = seq // tm
    kern = functools.partial(_proj_kernel, s5_w=s5_w, hg_w=hg_w, col_chunk=s5_w)
    return pl.pallas_call(
        kern,
        grid=(bsz, n_s),
        in_specs=[
            pl.BlockSpec((tm, d), lambda b, i: (b * n_s + i, 0)),
            pl.BlockSpec(w_in_b.shape, lambda b, i: (0, 0)),
        ],
        out_specs=[
            pl.BlockSpec((tm, s5_w), lambda b, i: (i, b)),
            pl.BlockSpec((tm, hg_w), lambda b, i: (b * n_s + i, 0)),
            pl.BlockSpec((tm, gate_w), lambda b, i: (b * n_s + i, 0)),
        ],
        out_shape=[
            jax.ShapeDtypeStruct((seq, bsz * s5_w), BF16),
            jax.ShapeDtypeStruct((n, hg_w), BF16),
            jax.ShapeDtypeStruct((n, gate_w), BF16),
        ],
        compiler_params=_cparams(("parallel", "parallel")),
        name="proj",
    )(x2, w_in_b)


def _s5_kernel(u_ref, bblk_ref, cblk_ref, lamr_ref, lami_ref, d_ref, wglu_ref, bglu_ref,
               y_ref, xs_ref, st_ref, *, bsz, tb, n_blk, half):
    @pl.when(pl.program_id(0) == 0)
    def _():
        st_ref[...] = jnp.zeros_like(st_ref)

    in_w = bblk_ref.shape[1]
    ys = []
    for j in range(n_blk):
        uj = u_ref[:, j * in_w:(j + 1) * in_w]
        xs_ref[...] = _dot(uj, bblk_ref[j])
        lr = jnp.broadcast_to(lamr_ref[j], (bsz, half))
        li = jnp.broadcast_to(lami_ref[j], (bsz, half))

        def body(t, carry, lr=lr, li=li):
            sr, si = carry
            rows = pl.ds(pl.multiple_of(t * bsz, bsz), bsz)
            nr = lr * sr - li * si + xs_ref[rows, :half]
            ni = lr * si + li * sr + xs_ref[rows, half:]
            xs_ref[rows, :half] = nr
            xs_ref[rows, half:] = ni
            return nr, ni

        sr, si = lax.fori_loop(0, tb, body, (st_ref[j, :, :half], st_ref[j, :, half:]), unroll=True)
        st_ref[j, :, :half] = sr
        st_ref[j, :, half:] = si
        ys.append(_dot(xs_ref[...].astype(BF16), cblk_ref[j]))
    y = jnp.concatenate(ys, axis=1) + d_ref[...] * u_ref[...].astype(F32)
    y = _gelu_tanh(y)
    z = _dot(y.astype(BF16), wglu_ref[...]) + bglu_ref[...]
    y_ref[...] = (y * _sigmoid(z)).astype(BF16)


def _s5_params(lam_re, lam_im, b_re, b_im, c_re, c_im, d_skip, log_step):
    g, p = lam_re.shape
    h = b_re.shape[-1]
    gb = S5_GROUPS_PER_BLOCK
    nb = g // gb
    lr, li = lam_re.astype(F32), lam_im.astype(F32)
    dt = jnp.exp(log_step.astype(F32))[:, None]
    mag = jnp.exp(lr * dt)
    lbr, lbi = mag * jnp.cos(li * dt), mag * jnp.sin(li * dt)
    den = lr * lr + li * li
    cr = ((lbr - 1.0) * lr + lbi * li) / den
    ci = (lbi * lr - (lbr - 1.0) * li) / den
    br, bi = b_re.astype(F32), b_im.astype(F32)
    bbar_r = cr[..., None] * br - ci[..., None] * bi
    bbar_i = cr[..., None] * bi + ci[..., None] * br
    eye = jnp.eye(gb, dtype=F32)

    def in_blk(m):
        m = m.reshape(nb, gb, p, h)
        return jnp.einsum('jgph,gk->jghkp', m, eye).reshape(nb, gb * h, gb * p)

    def out_blk(m):
        m = m.reshape(nb, gb, h, p)
        return jnp.einsum('jghp,gk->jgpkh', m, eye).reshape(nb, gb * p, gb * h)

    bblk = jnp.concatenate([in_blk(bbar_r), in_blk(bbar_i)], axis=2).astype(BF16)
    cblk = jnp.concatenate([out_blk(c_re.astype(F32)), out_blk(-c_im.astype(F32))], axis=1).astype(BF16)
    lamr = lbr.reshape(nb, 1, gb * p)
    lami = lbi.reshape(nb, 1, gb * p)
    return bblk, cblk, lamr, lami, d_skip.astype(F32).reshape(1, g * h)


def _s5(u_tm, params, w_glu_b, b_glu, bsz, seq, tb):
    bblk, cblk, lamr, lami, d_row = params
    n, w = u_tm.shape
    n_blk, in_w, two_half = bblk.shape
    half = two_half // 2
    rows = tb * bsz
    kern = functools.partial(_s5_kernel, bsz=bsz, tb=tb, n_blk=n_blk, half=half)
    const = lambda a: pl.BlockSpec(a.shape, lambda i: (0,) * a.ndim)
    return pl.pallas_call(
        kern,
        grid=(seq // tb,),
        in_specs=[pl.BlockSpec((rows, w), lambda i: (i, 0)),
                  const(bblk), const(cblk), const(lamr), const(lami), const(d_row),
                  const(w_glu_b), const(b_glu)],
        out_specs=pl.BlockSpec((rows, w), lambda i: (i, 0)),
        out_shape=jax.ShapeDtypeStruct((n, w), BF16),
        scratch_shapes=[pltpu.VMEM((rows, two_half), F32), pltpu.VMEM((n_blk, bsz, two_half), F32)],
        compiler_params=_cparams(("arbitrary",)),
        name="s5",
    )(u_tm, bblk, cblk, lamr, lami, d_row, w_glu_b, b_glu)


def _hgrn_tables(c):
    n_lev = int(math.log2(c))
    t = np.arange(c)[:, None]
    r = np.arange(c)[None, :]
    sel = []
    masks = []
    for lev in range(n_lev):
        w = c >> (lev + 1)
        pos = t % (2 * w)
        a = t - pos + w - 1
        upper = pos >= w
        sel.append(np.where(upper, (r > a) & (r <= t), (r > t) & (r <= a)))
        tt, ss = np.arange(c)[:, None], np.arange(c)[None, :]
        masks.append((tt // (2 * w) == ss // (2 * w)) & (tt % (2 * w) >= w) & (ss % (2 * w) < w))
    sel.append(r <= t)
    sel.append(r > t)
    masks.append(np.eye(c, dtype=bool))
    sel = np.concatenate(sel, axis=0).astype(np.float32)
    sel = np.concatenate([sel, sel], axis=1)
    masks = np.stack(masks, axis=0).astype(np.float32)
    return jnp.asarray(sel, BF16), jnp.asarray(masks, F32)


def _hgrn_kernel(hg_ref, sel_ref, mask_ref, lb_ref, gw_ref, y_ref, st_ref, *, c, n_chunks, width, heads):
    @pl.when(pl.program_id(1) == 0)
    def _():
        st_ref[...] = jnp.zeros_like(st_ref)

    n_lev = mask_ref.shape[0] - 1
    dh = width // heads
    lb = lb_ref[...]
    gw = gw_ref[...]

    def chunk(ci, carry):
        rows = pl.ds(pl.multiple_of(ci * c, c), c)
        q = hg_ref[0, rows, 0:width].astype(F32)
        fp = hg_ref[0, rows, width:2 * width].astype(F32)
        v = hg_ref[0, rows, 2 * width:3 * width]
        g = hg_ref[0, rows, 3 * width:4 * width].astype(F32)
        f = lb + (1.0 - lb) * _sigmoid(fp)
        lf = jnp.log(f)
        k = 1.0 - f
        qf = q * _sigmoid(q)
        h1 = lf.astype(BF16)
        h2 = (lf - h1.astype(F32)).astype(BF16)
        e_all = jnp.exp(_dot(sel_ref[...], jnp.concatenate([h1, h2], axis=0)))
        e_cum = e_all[n_lev * c:(n_lev + 1) * c]
        e_suf = e_all[(n_lev + 1) * c:(n_lev + 2) * c]
        outs = []
        for hd in range(heads):
            cols = slice(hd * dh, (hd + 1) * dh)
            qh, kh = qf[:, cols], k[:, cols]
            vh = v[:, cols]
            sc = mask_ref[n_lev] * _dot_nt(qh.astype(BF16), kh.astype(BF16))
            for lev in range(n_lev):
                el = e_all[lev * c:(lev + 1) * c, cols]
                sc = sc + mask_ref[lev] * _dot_nt((qh * el).astype(BF16), (kh * el).astype(BF16))
            st = st_ref[hd]
            o = _dot_nt((qh * e_cum[:, cols]).astype(BF16), st.astype(BF16)) + _dot(sc.astype(BF16), vh)
            e_tot = e_cum[c - 1:c, cols]
            st_ref[hd] = st * e_tot + _dot_tn(vh, (kh * e_suf[:, cols]).astype(BF16))
            o = o * lax.rsqrt(jnp.mean(o * o, axis=-1, keepdims=True) + RMS_EPS)
            outs.append(o)
        o = jnp.concatenate(outs, axis=1) * gw * (g * _sigmoid(g))
        y_ref[0, rows, :] = o.astype(BF16)
        return carry

    lax.fori_loop(0, n_chunks, chunk, 0, unroll=2)


def _hgrn(hg3, lb_row, gw_row, heads, tbh):
    bsz, seq, w4 = hg3.shape
    width = w4 // 4
    c = HGRN_CHUNK
    sel, masks = _hgrn_tables(c)
    kern = functools.partial(_hgrn_kernel, c=c, n_chunks=tbh // c, width=width, heads=heads)
    const = lambda a: pl.BlockSpec(a.shape, lambda b, i: (0,) * a.ndim)
    return pl.pallas_call(
        kern,
        grid=(bsz, seq // tbh),
        in_specs=[pl.BlockSpec((1, tbh, w4), lambda b, i: (b, i, 0)),
                  const(sel), const(masks), const(lb_row), const(gw_row)],
        out_specs=pl.BlockSpec((1, tbh, width), lambda b, i: (b, i, 0)),
        out_shape=jax.ShapeDtypeStruct((bsz, seq, width), BF16),
        scratch_shapes=[pltpu.VMEM((heads, width // heads, width // heads), F32)],
        compiler_params=_cparams(("parallel", "arbitrary")),
        name="hgrn",
    )(hg3, sel, masks, lb_row, gw_row)


def _merge_kernel(x_ref, ya_ref, yb_ref, gate_ref, wpa_ref, wpb_ref, wo_ref, g_ref, b_ref,
                  x1_ref, x1b_ref, *, alpha, d):
    ga = gate_ref[:, :d].astype(F32)
    gb = gate_ref[:, d:].astype(F32)
    merged = _sigmoid(ga) * _dot(ya_ref[...], wpa_ref[...]) + _sigmoid(gb) * _dot(yb_ref[...], wpb_ref[...])
    mix = _dot(merged.astype(BF16), wo_ref[...])
    x1 = _layer_norm_rows(alpha * x_ref[...] + mix, g_ref[...], b_ref[...])
    x1_ref[...] = x1
    x1b_ref[...] = x1.astype(BF16)


def _merge(x2, ya_tm, yb2, gates, wpa_b, wpb_b, wo_b, g_row, b_row, bsz, seq, alpha, tm):
    n, d = x2.shape
    w = yb2.shape[1]
    n_s = seq // tm
    kern = functools.partial(_merge_kernel, alpha=alpha, d=d)
    const = lambda a: pl.BlockSpec(a.shape, lambda b, i: (0,) * a.ndim)
    row = lambda cols: pl.BlockSpec((tm, cols), lambda b, i: (b * n_s + i, 0))
    return pl.pallas_call(
        kern,
        grid=(bsz, n_s),
        in_specs=[row(d),
                  pl.BlockSpec((tm, w), lambda b, i: (i, b)),
                  row(w), row(2 * d),
                  const(wpa_b), const(wpb_b), const(wo_b), const(g_row), const(b_row)],
        out_specs=[row(d), row(d)],
        out_shape=[jax.ShapeDtypeStruct((n, d), F32), jax.ShapeDtypeStruct((n, d), BF16)],
        compiler_params=_cparams(("parallel", "parallel")),
        name="merge",
    )(x2, ya_tm, yb2, gates, wpa_b, wpb_b, wo_b, g_row, b_row)


def _stage1(x, w_in, s5_lam_re, s5_lam_im, s5_b_re, s5_b_im, s5_c_re, s5_c_im, s5_d, s5_log_step,
            s5_w_glu, s5_b_glu, lb, gnorm_w, w_pa, w_pb, w_o, ln1_g, ln1_b, alpha):
    bsz, seq, d = x.shape
    n = bsz * seq
    s5_w = s5_w_glu.shape[0]
    hg_w = 4 * gnorm_w.shape[0]
    gate_w = 2 * d
    tm = min(512, seq)
    x2 = x.reshape(n, d)
    u_tm, hg, gates = _proj(x2, w_in.astype(BF16), bsz, seq, s5_w, hg_w, gate_w, tm)
    s5p = _s5_params(s5_lam_re, s5_lam_im, s5_b_re, s5_b_im, s5_c_re, s5_c_im, s5_d, s5_log_step)
    ya = _s5(u_tm.reshape(seq * bsz, s5_w), s5p, s5_w_glu.astype(BF16),
             s5_b_glu.astype(F32).reshape(1, s5_w), bsz, seq, tb=min(128, seq))
    yb = _hgrn(hg.reshape(bsz, seq, hg_w), lb.reshape(1, -1), gnorm_w.astype(F32).reshape(1, -1),
               heads=gnorm_w.shape[0] // HGRN_HEAD_DIM, tbh=min(512, seq))
    return _merge(x2, ya.reshape(seq, bsz * s5_w), yb.reshape(n, -1), gates,
                  w_pa.astype(BF16), w_pb.astype(BF16), w_o.astype(BF16),
                  ln1_g.astype(F32).reshape(1, d), ln1_b.astype(F32).reshape(1, d), bsz, seq, alpha, tm)


RANK_BASE = 2.0 ** 100


def _top_sorted(s, k):
    rows = []
    cur = s
    for j in range(k):
        mx = jnp.max(cur, axis=0, keepdims=True)
        rows.append(mx)
        cur = jnp.where(cur == mx, -(j + 1) * RANK_BASE, cur)
    rank = jnp.where(cur < -0.5 * RANK_BASE, cur * (-1.0 / RANK_BASE) - 1.0, float(k))
    return rows, rank


def _route_kernel(xb_ref, wq_ref, keys_ref, cnt_ref, p1_ref, r2_ref, p2_ref, s_scr, *, heads, topk):
    nk = keys_ref.shape[1]

    q_t = _dot_nt(wq_ref[...], xb_ref[...])
    for hp in range(2 * heads):
        s_scr[hp] = _dot(keys_ref[hp], q_t[hp * nk:(hp + 1) * nk, :].astype(BF16))

    def head(h, carry):
        s1 = s_scr[2 * h]
        s2 = s_scr[2 * h + 1]
        a, rank1 = _top_sorted(s1, topk)
        b, rank2 = _top_sorted(s2, topk)
        a_all = jnp.concatenate(a, axis=0)
        b_all = jnp.concatenate(b, axis=0)
        half_rows = topk // 2
        j_idx = lax.broadcasted_iota(jnp.int32, (half_rows, 1), 0)
        cands = [a_all + b[0]]
        for l in range(1, half_rows):
            cands.append(jnp.where(j_idx < topk // (l + 1), a_all[:half_rows] + b[l], -jnp.inf))
        cands.append(a[0] + b_all[half_rows:])
        cur = jnp.concatenate(cands, axis=0)
        m = a[0] + b[0]
        z = jnp.zeros_like(m)
        tau = m
        for r in range(topk):
            tau = jnp.max(cur, axis=0, keepdims=True)
            z = z + jnp.exp(tau - m)
            if r + 1 < topk:
                cur = jnp.where(cur == tau, -jnp.inf, cur)
        hit = lambda c: jnp.where(c >= tau, 1.0, 0.0)
        cnt_lo = hit(cands[0][:half_rows])
        for l in range(1, half_rows):
            cnt_lo = cnt_lo + hit(cands[l])
        cnt_0 = cnt_lo[0:1] + jnp.sum(hit(cands[half_rows]), axis=0, keepdims=True)
        cnt_rows = jnp.concatenate([cnt_lo, hit(cands[0][half_rows:])], axis=0)
        cnt = jnp.zeros(s1.shape, F32)
        for c in range(half_rows):
            ranks_with = jnp.sum(jnp.where(cnt_rows > float(c), 1.0, 0.0), axis=0, keepdims=True)
            cnt = jnp.where(rank1 < ranks_with, float(c + 1), cnt)
        cnt = jnp.where(rank1 == 0.0, jnp.maximum(cnt, cnt_0), cnt)
        cnt_ref[h] = cnt.astype(BF16)
        p1_ref[h] = jnp.exp(s1 - a[0]).astype(BF16)
        r2_ref[h] = rank2.astype(BF16)
        p2_ref[h] = (jnp.exp(s2 - b[0]) / z).astype(BF16)
        return carry

    lax.fori_loop(0, heads, head, 0)


def _route(x1b, wq_t, keys, tt):
    n, d = x1b.shape
    hp, nk, _ = keys.shape
    heads = hp // 2
    kern = functools.partial(_route_kernel, heads=heads, topk=PEER_TOPK)
    out = lambda dt: jax.ShapeDtypeStruct((heads, nk, n), dt)
    ospec = pl.BlockSpec((heads, nk, tt), lambda i: (0, 0, i))
    return pl.pallas_call(
        kern,
        grid=(n // tt,),
        in_specs=[pl.BlockSpec((tt, d), lambda i: (i, 0)),
                  pl.BlockSpec(wq_t.shape, lambda i: (0, 0)),
                  pl.BlockSpec(keys.shape, lambda i: (0, 0, 0))],
        out_specs=[ospec, ospec, ospec, ospec],
        out_shape=[out(BF16), out(BF16), out(BF16), out(BF16)],
        scratch_shapes=[pltpu.VMEM((hp, nk, tt), F32)],
        compiler_params=_cparams(("parallel",)),
        name="route",
    )(x1b, wq_t, keys)


PEER_DOT_SPLITS = (2, 6, 8, 8, 8)
PEER_CHUNK_I1 = sum(PEER_DOT_SPLITS)


def _peer_kernel(xb_ref, x_ref, u_ref, vt_ref, cnt_ref, p1_ref, r2_ref, p2_ref, g_ref, b_ref,
                 out_ref, acc_ref, h_ref, *, heads, nk, alpha):
    j = pl.program_id(1)
    tt = xb_ref.shape[0]
    n_i1 = u_ref.shape[0] // nk
    pk = BF16_SUBLANES
    tiles = nk // pk

    @pl.when(j == 0)
    def _():
        acc_ref[...] = jnp.zeros_like(acc_ref)

    cnt_t = [cnt_ref[h].astype(F32) for h in range(heads)]
    p1_t = [p1_ref[h].astype(F32) for h in range(heads)]

    def expert_dots(i0, n):
        h = _dot_nt(u_ref[i0 * nk:(i0 + n) * nk, :], xb_ref[...])
        h_ref[i0 * tiles:(i0 + n) * tiles] = h.astype(BF16).reshape(n * tiles, pk, tt)

    def gate_slab(ii):
        w = jnp.zeros((tiles, pk, tt), BF16)
        for h in range(heads):
            cnt = jnp.broadcast_to(cnt_t[h][ii:ii + 1], (pk, tt)).astype(BF16)
            p1 = jnp.broadcast_to(p1_t[h][ii:ii + 1], (pk, tt)).astype(BF16)
            r2 = r2_ref[h].reshape(tiles, pk, tt)
            p2 = p2_ref[h].reshape(tiles, pk, tt)
            w = w + jnp.where(r2 < cnt[None], p2, 0.0) * p1[None]
        h_ref[ii * tiles:(ii + 1) * tiles] = w * _gelu_tanh(h_ref[ii * tiles:(ii + 1) * tiles])

    starts = [sum(PEER_DOT_SPLITS[:k]) for k in range(len(PEER_DOT_SPLITS))]
    expert_dots(0, PEER_DOT_SPLITS[0])
    for k, n in enumerate(PEER_DOT_SPLITS):
        if k + 1 < len(PEER_DOT_SPLITS):
            expert_dots(starts[k + 1], PEER_DOT_SPLITS[k + 1])
        for ii in range(starts[k], starts[k] + n):
            gate_slab(ii)
    acc_ref[...] += _dot(vt_ref[...], h_ref[...].reshape(n_i1 * nk, tt))

    @pl.when(j == pl.num_programs(1) - 1)
    def _():
        z = alpha * x_ref[...] + acc_ref[...].T
        out_ref[...] = _layer_norm_rows(z, g_ref[...], b_ref[...])


def _peer(x1, x1b, u_b, vt_b, routing, g_row, b_row, alpha, tt, chunk):
    n, d = x1.shape
    cnt, p1, r2, p2 = routing
    heads, nk, _ = cnt.shape
    n_exp = u_b.shape[0]
    assert chunk == PEER_CHUNK_I1 * nk and PEER_CHUNK_I1 % BF16_SUBLANES == 0, "whole packed tiles of i1 rows per step"
    kern = functools.partial(_peer_kernel, heads=heads, nk=nk, alpha=alpha)
    rspec = pl.BlockSpec((heads, nk, tt), lambda i, j: (0, 0, i))
    cspec = pl.BlockSpec((heads, PEER_CHUNK_I1, tt), lambda i, j: (0, j, i))
    packed = (chunk // BF16_SUBLANES, BF16_SUBLANES, tt)
    return pl.pallas_call(
        kern,
        grid=(n // tt, n_exp // chunk),
        in_specs=[pl.BlockSpec((tt, d), lambda i, j: (i, 0)),
                  pl.BlockSpec((tt, d), lambda i, j: (i, 0)),
                  pl.BlockSpec((chunk, d), lambda i, j: (j, 0)),
                  pl.BlockSpec((d, chunk), lambda i, j: (0, j)),
                  cspec, cspec, rspec, rspec,
                  pl.BlockSpec((1, d), lambda i, j: (0, 0)),
                  pl.BlockSpec((1, d), lambda i, j: (0, 0))],
        out_specs=pl.BlockSpec((tt, d), lambda i, j: (i, 0)),
        out_shape=jax.ShapeDtypeStruct((n, d), F32),
        scratch_shapes=[pltpu.VMEM((d, tt), F32), pltpu.VMEM(packed, BF16)],
        compiler_params=_cparams(("parallel", "arbitrary")),
        name="peer",
    )(x1b, x1, u_b, vt_b, cnt, p1, r2, p2, g_row, b_row)


def _stage2(x1, x1b, w_pq, sub_keys, peer_u, peer_v, ln2_g, ln2_b, alpha):
    n, d = x1.shape
    heads, _, nk, half = sub_keys.shape
    tt = min(512, n)
    routing = _route(x1b, w_pq.T.astype(BF16), sub_keys.reshape(heads * 2, nk, half).astype(BF16), tt)
    return _peer(x1, x1b, peer_u.astype(BF16), peer_v.T.astype(BF16), routing,
                 ln2_g.astype(F32).reshape(1, d), ln2_b.astype(F32).reshape(1, d), alpha, tt, chunk=PEER_CHUNK_I1 * nk)


def kernel(x, w_in, s5_lam_re, s5_lam_im, s5_b_re, s5_b_im, s5_c_re, s5_c_im, s5_d, s5_log_step, s5_w_glu,
           s5_b_glu, hgrn_lb_logits, hgrn_gnorm_w, w_pa, w_pb, w_o, ln1_g, ln1_b, peer_w_q, peer_sub_keys,
           peer_u, peer_v, ln2_g, ln2_b):
    depth = w_in.shape[0]
    alpha = (2.0 * depth) ** 0.25
    lb_all = jnp.cumsum(jax.nn.softmax(hgrn_lb_logits.astype(F32), axis=0), axis=0)[:depth]
    bsz, seq, d = x.shape
    for l in range(depth):
        x1, x1b = _stage1(x, w_in[l], s5_lam_re[l], s5_lam_im[l], s5_b_re[l], s5_b_im[l], s5_c_re[l], s5_c_im[l],
                          s5_d[l], s5_log_step[l], s5_w_glu[l], s5_b_glu[l], lb_all[l], hgrn_gnorm_w[l],
                          w_pa[l], w_pb[l], w_o[l], ln1_g[l], ln1_b[l], alpha)
        x = _stage2(x1, x1b, peer_w_q[l], peer_sub_keys[l], peer_u[l], peer_v[l], ln2_g[l], ln2_b[l],
                    alpha).reshape(bsz, seq, d)
    return x
```

```python
import functools
import math

import numpy as np
import jax
import jax.numpy as jnp
from jax import lax
from jax.experimental import pallas as pl
from jax.experimental.pallas import tpu as pltpu

F32 = jnp.float32
BF16 = jnp.bfloat16

V7X_VMEM_LIMIT_BYTES = 58 * 1024 * 1024
BF16_SUBLANES = 16

S5_GROUPS_PER_BLOCK = 8
HGRN_HEAD_DIM = 128
HGRN_CHUNK = 128
PEER_TOPK = 16
LN_EPS = 1e-5
RMS_EPS = 1e-6


def _cparams(semantics):
    return pltpu.CompilerParams(dimension_semantics=semantics, vmem_limit_bytes=V7X_VMEM_LIMIT_BYTES)


def _dot(a, b):
    return jnp.dot(a, b, preferred_element_type=F32)


def _dot_nt(a, b):
    return lax.dot_general(a, b, (((1,), (1,)), ((), ())), preferred_element_type=F32)


def _dot_tn(a, b):
    return lax.dot_general(a, b, (((0,), (0,)), ((), ())), preferred_element_type=F32)


def _sigmoid(x):
    return 1.0 / (1.0 + jnp.exp(-x))


def _gelu_tanh(x):
    c = math.sqrt(2.0 / math.pi)
    return 0.5 * x * (1.0 + jnp.tanh(c * (x + 0.044715 * (x * x * x))))


def _layer_norm_rows(z, g, b):
    mu = jnp.mean(z, axis=-1, keepdims=True)
    zc = z - mu
    var = jnp.mean(zc * zc, axis=-1, keepdims=True)
    return zc * lax.rsqrt(var + LN_EPS) * g + b


def _proj_kernel(x_ref, w_ref, u_ref, hg_ref, gate_ref, *, s5_w, hg_w, col_chunk):
    xb = x_ref[...].astype(BF16)
    n_cols = w_ref.shape[1]
    for c0 in range(0, n_cols, col_chunk):
        p = _dot(xb, w_ref[:, c0:c0 + col_chunk]).astype(BF16)
        if c0 < s5_w:
            u_ref[:, c0:c0 + col_chunk] = p
        elif c0 < s5_w + hg_w:
            hg_ref[:, c0 - s5_w:c0 - s5_w + col_chunk] = p
        else:
            o = c0 - s5_w - hg_w
            gate_ref[:, o:o + col_chunk] = p


def _proj(x2, w_in_b, bsz, seq, s5_w, hg_w, gate_w, tm):
    n, d = x2.shape
    n_s = seq // tm
    kern = functools.partial(_proj_kernel, s5_w=s5_w, hg_w=hg_w, col_chunk=s5_w)
    return pl.pallas_call(
        kern,
        grid=(bsz, n_s),
        in_specs=[
            pl.BlockSpec((tm, d), lambda b, i: (b * n_s + i, 0)),
            pl.BlockSpec(w_in_b.shape, lambda b, i: (0, 0)),
        ],
        out_specs=[
            pl.BlockSpec((tm, s5_w), lambda b, i: (i, b)),
            pl.BlockSpec((tm, hg_w), lambda b, i: (b * n_s + i, 0)),
            pl.BlockSpec((tm, gate_w), lambda b, i: (b * n_s + i, 0)),
        ],
        out_shape=[
            jax.ShapeDtypeStruct((seq, bsz * s5_w), BF16),
            jax.ShapeDtypeStruct((n, hg_w), BF16),
            jax.ShapeDtypeStruct((n, gate_w), BF16),
        ],
        compiler_params=_cparams(("parallel", "parallel")),
        name="proj",
    )(x2, w_in_b)


def _s5_kernel(u_ref, bblk_ref, cblk_ref, lamr_ref, lami_ref, d_ref, wglu_ref, bglu_ref,
               y_ref, xs_ref, st_ref, *, bsz, tb, n_blk, half):
    @pl.when(pl.program_id(0) == 0)
    def _():
        st_ref[...] = jnp.zeros_like(st_ref)

    in_w = bblk_ref.shape[1]
    ys = []
    for j in range(n_blk):
        uj = u_ref[:, j * in_w:(j + 1) * in_w]
        xs_ref[...] = _dot(uj, bblk_ref[j])
        lr = jnp.broadcast_to(lamr_ref[j], (bsz, half))
        li = jnp.broadcast_to(lami_ref[j], (bsz, half))

        def body(t, carry, lr=lr, li=li):
            sr, si = carry
            rows = pl.ds(pl.multiple_of(t * bsz, bsz), bsz)
            nr = lr * sr - li * si + xs_ref[rows, :half]
            ni = lr * si + li * sr + xs_ref[rows, half:]
            xs_ref[rows, :half] = nr
            xs_ref[rows, half:] = ni
            return nr, ni

        sr, si = lax.fori_loop(0, tb, body, (st_ref[j, :, :half], st_ref[j, :, half:]), unroll=True)
        st_ref[j, :, :half] = sr
        st_ref[j, :, half:] = si
        ys.append(_dot(xs_ref[...].astype(BF16), cblk_ref[j]))
    y = jnp.concatenate(ys, axis=1) + d_ref[...] * u_ref[...].astype(F32)
    y = _gelu_tanh(y)
    z = _dot(y.astype(BF16), wglu_ref[...]) + bglu_ref[...]
    y_ref[...] = (y * _sigmoid(z)).astype(BF16)


def _s5_params(lam_re, lam_im, b_re, b_im, c_re, c_im, d_skip, log_step):
    g, p = lam_re.shape
    h = b_re.shape[-1]
    gb = S5_GROUPS_PER_BLOCK
    nb = g // gb
    lr, li = lam_re.astype(F32), lam_im.astype(F32)
    dt = jnp.exp(log_step.astype(F32))[:, None]
    mag = jnp.exp(lr * dt)
    lbr, lbi = mag * jnp.cos(li * dt), mag * jnp.sin(li * dt)
    den = lr * lr + li * li
    cr = ((lbr - 1.0) * lr + lbi * li) / den
    ci = (lbi * lr - (lbr - 1.0) * li) / den
    br, bi = b_re.astype(F32), b_im.astype(F32)
    bbar_r = cr[..., None] * br - ci[..., None] * bi
    bbar_i = cr[..., None] * bi + ci[..., None] * br
    eye = jnp.eye(gb, dtype=F32)

    def in_blk(m):
        m = m.reshape(nb, gb, p, h)
        return jnp.einsum('jgph,gk->jghkp', m, eye).reshape(nb, gb * h, gb * p)

    def out_blk(m):
        m = m.reshape(nb, gb, h, p)
        return jnp.einsum('jghp,gk->jgpkh', m, eye).reshape(nb, gb * p, gb * h)

    bblk = jnp.concatenate([in_blk(bbar_r), in_blk(bbar_i)], axis=2).astype(BF16)
    cblk = jnp.concatenate([out_blk(c_re.astype(F32)), out_blk(-c_im.astype(F32))], axis=1).astype(BF16)
    lamr = lbr.reshape(nb, 1, gb * p)
    lami = lbi.reshape(nb, 1, gb * p)
    return bblk, cblk, lamr, lami, d_skip.astype(F32).reshape(1, g * h)


def _s5(u_tm, params, w_glu_b, b_glu, bsz, seq, tb):
    bblk, cblk, lamr, lami, d_row = params
    n, w = u_tm.shape
    n_blk, in_w, two_half = bblk.shape
    half = two_half // 2
    rows = tb * bsz
    kern = functools.partial(_s5_kernel, bsz=bsz, tb=tb, n_blk=n_blk, half=half)
    const = lambda a: pl.BlockSpec(a.shape, lambda i: (0,) * a.ndim)
    return pl.pallas_call(
        kern,
        grid=(seq // tb,),
        in_specs=[pl.BlockSpec((rows, w), lambda i: (i, 0)),
                  const(bblk), const(cblk), const(lamr), const(lami), const(d_row),
                  const(w_glu_b), const(b_glu)],
        out_specs=pl.BlockSpec((rows, w), lambda i: (i, 0)),
        out_shape=jax.ShapeDtypeStruct((n, w), BF16),
        scratch_shapes=[pltpu.VMEM((rows, two_half), F32), pltpu.VMEM((n_blk, bsz, two_half), F32)],
        compiler_params=_cparams(("arbitrary",)),
        name="s5",
    )(u_tm, bblk, cblk, lamr, lami, d_row, w_glu_b, b_glu)


def _hgrn_tables(c):
    n_lev = int(math.log2(c))
    t = np.arange(c)[:, None]
    r = np.arange(c)[None, :]
    sel = []
    masks = []
    for lev in range(n_lev):
        w = c >> (lev + 1)
        pos = t % (2 * w)
        a = t - pos + w - 1
        upper = pos >= w
        sel.append(np.where(upper, (r > a) & (r <= t), (r > t) & (r <= a)))
        tt, ss = np.arange(c)[:, None], np.arange(c)[None, :]
        masks.append((tt // (2 * w) == ss // (2 * w)) & (tt % (2 * w) >= w) & (ss % (2 * w) < w))
    sel.append(r <= t)
    sel.append(r > t)
    masks.append(np.eye(c, dtype=bool))
    sel = np.concatenate(sel, axis=0).astype(np.float32)
    sel = np.concatenate([sel, sel], axis=1)
    masks = np.stack(masks, axis=0).astype(np.float32)
    return jnp.asarray(sel, BF16), jnp.asarray(masks, F32)


def _hgrn_kernel(hg_ref, sel_ref, mask_ref, lb_ref, gw_ref, y_ref, st_ref, *, c, n_chunks, width, heads):
    @pl.when(pl.program_id(1) == 0)
    def _():
        st_ref[...] = jnp.zeros_like(st_ref)

    n_lev = mask_ref.shape[0] - 1
    dh = width // heads
    lb = lb_ref[...]
    gw = gw_ref[...]

    def chunk(ci, carry):
        rows = pl.ds(pl.multiple_of(ci * c, c), c)
        q = hg_ref[0, rows, 0:width].astype(F32)
        fp = hg_ref[0, rows, width:2 * width].astype(F32)
        v = hg_ref[0, rows, 2 * width:3 * width]
        g = hg_ref[0, rows, 3 * width:4 * width].astype(F32)
        f = lb + (1.0 - lb) * _sigmoid(fp)
        lf = jnp.log(f)
        k = 1.0 - f
        qf = q * _sigmoid(q)
        h1 = lf.astype(BF16)
        h2 = (lf - h1.astype(F32)).astype(BF16)
        e_all = jnp.exp(_dot(sel_ref[...], jnp.concatenate([h1, h2], axis=0)))
        e_cum = e_all[n_lev * c:(n_lev + 1) * c]
        e_suf = e_all[(n_lev + 1) * c:(n_lev + 2) * c]
        outs = []
        for hd in range(heads):
            cols = slice(hd * dh, (hd + 1) * dh)
            qh, kh = qf[:, cols], k[:, cols]
            vh = v[:, cols]
            sc = mask_ref[n_lev] * _dot_nt(qh.astype(BF16), kh.astype(BF16))
            for lev in range(n_lev):
                el = e_all[lev * c:(lev + 1) * c, cols]
                sc = sc + mask_ref[lev] * _dot_nt((qh * el).astype(BF16), (kh * el).astype(BF16))
            st = st_ref[hd]
            o = _dot_nt((qh * e_cum[:, cols]).astype(BF16), st.astype(BF16)) + _dot(sc.astype(BF16), vh)
            e_tot = e_cum[c - 1:c, cols]
            st_ref[hd] = st * e_tot + _dot_tn(vh, (kh * e_suf[:, cols]).astype(BF16))
            o = o * lax.rsqrt(jnp.mean(o * o, axis=-1, keepdims=True) + RMS_EPS)
            outs.append(o)
        o = jnp.concatenate(outs, axis=1) * gw * (g * _sigmoid(g))
        y_ref[0, rows, :] = o.astype(BF16)
        return carry

    lax.fori_loop(0, n_chunks, chunk, 0, unroll=True)


def _hgrn(hg3, lb_row, gw_row, heads, tbh):
    bsz, seq, w4 = hg3.shape
    width = w4 // 4
    c = HGRN_CHUNK
    sel, masks = _hgrn_tables(c)
    kern = functools.partial(_hgrn_kernel, c=c, n_chunks=tbh // c, width=width, heads=heads)
    const = lambda a: pl.BlockSpec(a.shape, lambda b, i: (0,) * a.ndim)
    return pl.pallas_call(
        kern,
        grid=(bsz, seq // tbh),
        in_specs=[pl.BlockSpec((1, tbh, w4), lambda b, i: (b, i, 0)),
                  const(sel), const(masks), const(lb_row), const(gw_row)],
        out_specs=pl.BlockSpec((1, tbh, width), lambda b, i: (b, i, 0)),
        out_shape=jax.ShapeDtypeStruct((bsz, seq, width), BF16),
        scratch_shapes=[pltpu.VMEM((heads, width // heads, width // heads), F32)],
        compiler_params=_cparams(("parallel", "arbitrary")),
        name="hgrn",
    )(hg3, sel, masks, lb_row, gw_row)


def _merge_kernel(x_ref, ya_ref, yb_ref, gate_ref, wpa_ref, wpb_ref, wo_ref, g_ref, b_ref,
                  x1_ref, x1b_ref, *, alpha, d):
    ga = gate_ref[:, :d].astype(F32)
    gb = gate_ref[:, d:].astype(F32)
    merged = _sigmoid(ga) * _dot(ya_ref[...], wpa_ref[...]) + _sigmoid(gb) * _dot(yb_ref[...], wpb_ref[...])
    mix = _dot(merged.astype(BF16), wo_ref[...])
    x1 = _layer_norm_rows(alpha * x_ref[...] + mix, g_ref[...], b_ref[...])
    x1_ref[...] = x1
    x1b_ref[...] = x1.astype(BF16)


def _merge(x2, ya_tm, yb2, gates, wpa_b, wpb_b, wo_b, g_row, b_row, bsz, seq, alpha, tm):
    n, d = x2.shape
    w = yb2.shape[1]
    n_s = seq // tm
    kern = functools.partial(_merge_kernel, alpha=alpha, d=d)
    const = lambda a: pl.BlockSpec(a.shape, lambda b, i: (0,) * a.ndim)
    row = lambda cols: pl.BlockSpec((tm, cols), lambda b, i: (b * n_s + i, 0))
    return pl.pallas_call(
        kern,
        grid=(bsz, n_s),
        in_specs=[row(d),
                  pl.BlockSpec((tm, w), lambda b, i: (i, b)),
                  row(w), row(2 * d),
                  const(wpa_b), const(wpb_b), const(wo_b), const(g_row), const(b_row)],
        out_specs=[row(d), row(d)],
        out_shape=[jax.ShapeDtypeStruct((n, d), F32), jax.ShapeDtypeStruct((n, d), BF16)],
        compiler_params=_cparams(("parallel", "parallel")),
        name="merge",
    )(x2, ya_tm, yb2, gates, wpa_b, wpb_b, wo_b, g_row, b_row)


def _stage1(x, w_in, s5_lam_re, s5_lam_im, s5_b_re, s5_b_im, s5_c_re, s5_c_im, s5_d, s5_log_step,
            s5_w_glu, s5_b_glu, lb, gnorm_w, w_pa, w_pb, w_o, ln1_g, ln1_b, alpha):
    bsz, seq, d = x.shape
    n = bsz * seq
    s5_w = s5_w_glu.shape[0]
    hg_w = 4 * gnorm_w.shape[0]
    gate_w = 2 * d
    tm = min(512, seq)
    x2 = x.reshape(n, d)
    u_tm, hg, gates = _proj(x2, w_in.astype(BF16), bsz, seq, s5_w, hg_w, gate_w, tm)
    s5p = _s5_params(s5_lam_re, s5_lam_im, s5_b_re, s5_b_im, s5_c_re, s5_c_im, s5_d, s5_log_step)
    ya = _s5(u_tm.reshape(seq * bsz, s5_w), s5p, s5_w_glu.astype(BF16),
             s5_b_glu.astype(F32).reshape(1, s5_w), bsz, seq, tb=min(128, seq))
    yb = _hgrn(hg.reshape(bsz, seq, hg_w), lb.reshape(1, -1), gnorm_w.astype(F32).reshape(1, -1),
               heads=gnorm_w.shape[0] // HGRN_HEAD_DIM, tbh=min(512, seq))
    return _merge(x2, ya.reshape(seq, bsz * s5_w), yb.reshape(n, -1), gates,
                  w_pa.astype(BF16), w_pb.astype(BF16), w_o.astype(BF16),
                  ln1_g.astype(F32).reshape(1, d), ln1_b.astype(F32).reshape(1, d), bsz, seq, alpha, tm)


RANK_BASE = 2.0 ** 100


def _top_sorted(s, k):
    rows = []
    cur = s
    for j in range(k):
        mx = jnp.max(cur, axis=0, keepdims=True)
        rows.append(mx)
        cur = jnp.where(cur == mx, -(j + 1) * RANK_BASE, cur)
    rank = jnp.where(cur < -0.5 * RANK_BASE, cur * (-1.0 / RANK_BASE) - 1.0, float(k))
    return rows, rank


def _route_kernel(xb_ref, wq_ref, keys_ref, cnt_ref, p1_ref, r2_ref, p2_ref, s_scr, *, heads, topk):
    nk = keys_ref.shape[1]

    q_t = _dot_nt(wq_ref[...], xb_ref[...])
    for hp in range(2 * heads):
        s_scr[hp] = _dot(keys_ref[hp], q_t[hp * nk:(hp + 1) * nk, :].astype(BF16))

    def head(h, carry):
        s1 = s_scr[2 * h]
        s2 = s_scr[2 * h + 1]
        a, rank1 = _top_sorted(s1, topk)
        b, rank2 = _top_sorted(s2, topk)
        a_all = jnp.concatenate(a, axis=0)
        b_all = jnp.concatenate(b, axis=0)
        half_rows = topk // 2
        j_idx = lax.broadcasted_iota(jnp.int32, (half_rows, 1), 0)
        cands = [a_all + b[0]]
        for l in range(1, half_rows):
            cands.append(jnp.where(j_idx < topk // (l + 1), a_all[:half_rows] + b[l], -jnp.inf))
        cands.append(a[0] + b_all[half_rows:])
        cur = jnp.concatenate(cands, axis=0)
        m = a[0] + b[0]
        z = jnp.zeros_like(m)
        tau = m
        for r in range(topk):
            tau = jnp.max(cur, axis=0, keepdims=True)
            z = z + jnp.exp(tau - m)
            if r + 1 < topk:
                cur = jnp.where(cur == tau, -jnp.inf, cur)
        hit = lambda c: jnp.where(c >= tau, 1.0, 0.0)
        cnt_lo = hit(cands[0][:half_rows])
        for l in range(1, half_rows):
            cnt_lo = cnt_lo + hit(cands[l])
        cnt_0 = cnt_lo[0:1] + jnp.sum(hit(cands[half_rows]), axis=0, keepdims=True)
        cnt_rows = jnp.concatenate([cnt_lo, hit(cands[0][half_rows:])], axis=0)
        cnt = jnp.zeros(s1.shape, F32)
        for c in range(half_rows):
            ranks_with = jnp.sum(jnp.where(cnt_rows > float(c), 1.0, 0.0), axis=0, keepdims=True)
            cnt = jnp.where(rank1 < ranks_with, float(c + 1), cnt)
        cnt = jnp.where(rank1 == 0.0, jnp.maximum(cnt, cnt_0), cnt)
        cnt_ref[h] = cnt.astype(BF16)
        p1_ref[h] = jnp.exp(s1 - a[0]).astype(BF16)
        r2_ref[h] = rank2.astype(BF16)
        p2_ref[h] = (jnp.exp(s2 - b[0]) / z).astype(BF16)
        return carry

    lax.fori_loop(0, heads, head, 0)


def _route(x1b, wq_t, keys, tt):
    n, d = x1b.shape
    hp, nk, _ = keys.shape
    heads = hp // 2
    kern = functools.partial(_route_kernel, heads=heads, topk=PEER_TOPK)
    out = lambda dt: jax.ShapeDtypeStruct((heads, nk, n), dt)
    ospec = pl.BlockSpec((heads, nk, tt), lambda i: (0, 0, i))
    return pl.pallas_call(
        kern,
        grid=(n // tt,),
        in_specs=[pl.BlockSpec((tt, d), lambda i: (i, 0)),
                  pl.BlockSpec(wq_t.shape, lambda i: (0, 0)),
                  pl.BlockSpec(keys.shape, lambda i: (0, 0, 0))],
        out_specs=[ospec, ospec, ospec, ospec],
        out_shape=[out(BF16), out(BF16), out(BF16), out(BF16)],
        scratch_shapes=[pltpu.VMEM((hp, nk, tt), F32)],
        compiler_params=_cparams(("parallel",)),
        name="route",
    )(x1b, wq_t, keys)


PEER_DOT_SPLITS = (2, 6, 8, 8, 8)
PEER_CHUNK_I1 = sum(PEER_DOT_SPLITS)


def _peer_kernel(xb_ref, x_ref, u_ref, vt_ref, cnt_ref, p1_ref, r2_ref, p2_ref, g_ref, b_ref,
                 out_ref, acc_ref, h_ref, *, heads, nk, alpha):
    j = pl.program_id(1)
    tt = xb_ref.shape[0]
    n_i1 = u_ref.shape[0] // nk
    pk = BF16_SUBLANES
    tiles = nk // pk

    @pl.when(j == 0)
    def _():
        acc_ref[...] = jnp.zeros_like(acc_ref)

    cnt_t = [cnt_ref[h].astype(F32) for h in range(heads)]
    p1_t = [p1_ref[h].astype(F32) for h in range(heads)]

    def expert_dots(i0, n):
        h = _dot_nt(u_ref[i0 * nk:(i0 + n) * nk, :], xb_ref[...])
        h_ref[i0 * tiles:(i0 + n) * tiles] = h.astype(BF16).reshape(n * tiles, pk, tt)

    def gate_slab(ii):
        w = jnp.zeros((tiles, pk, tt), BF16)
        for h in range(heads):
            cnt = jnp.broadcast_to(cnt_t[h][ii:ii + 1], (pk, tt)).astype(BF16)
            p1 = jnp.broadcast_to(p1_t[h][ii:ii + 1], (pk, tt)).astype(BF16)
            r2 = r2_ref[h].reshape(tiles, pk, tt)
            p2 = p2_ref[h].reshape(tiles, pk, tt)
            w = w + jnp.where(r2 < cnt[None], p2, 0.0) * p1[None]
        h_ref[ii * tiles:(ii + 1) * tiles] = w * _gelu_tanh(h_ref[ii * tiles:(ii + 1) * tiles])

    starts = [sum(PEER_DOT_SPLITS[:k]) for k in range(len(PEER_DOT_SPLITS))]
    expert_dots(0, PEER_DOT_SPLITS[0])
    for k, n in enumerate(PEER_DOT_SPLITS):
        if k + 1 < len(PEER_DOT_SPLITS):
            expert_dots(starts[k + 1], PEER_DOT_SPLITS[k + 1])
        for ii in range(starts[k], starts[k] + n):
            gate_slab(ii)
    acc_ref[...] += _dot(vt_ref[...], h_ref[...].reshape(n_i1 * nk, tt))

    @pl.when(j == pl.num_programs(1) - 1)
    def _():
        z = alpha * x_ref[...] + acc_ref[...].T
        out_ref[...] = _layer_norm_rows(z, g_ref[...], b_ref[...])


def _peer(x1, x1b, u_b, vt_b, routing, g_row, b_row, alpha, tt, chunk):
    n, d = x1.shape
    cnt, p1, r2, p2 = routing
    heads, nk, _ = cnt.shape
    n_exp = u_b.shape[0]
    assert chunk == PEER_CHUNK_I1 * nk and PEER_CHUNK_I1 % BF16_SUBLANES == 0, "whole packed tiles of i1 rows per step"
    kern = functools.partial(_peer_kernel, heads=heads, nk=nk, alpha=alpha)
    rspec = pl.BlockSpec((heads, nk, tt), lambda i, j: (0, 0, i))
    cspec = pl.BlockSpec((heads, PEER_CHUNK_I1, tt), lambda i, j: (0, j, i))
    packed = (chunk // BF16_SUBLANES, BF16_SUBLANES, tt)
    return pl.pallas_call(
        kern,
        grid=(n // tt, n_exp // chunk),
        in_specs=[pl.BlockSpec((tt, d), lambda i, j: (i, 0)),
                  pl.BlockSpec((tt, d), lambda i, j: (i, 0)),
                  pl.BlockSpec((chunk, d), lambda i, j: (j, 0)),
                  pl.BlockSpec((d, chunk), lambda i, j: (0, j)),
                  cspec, cspec, rspec, rspec,
                  pl.BlockSpec((1, d), lambda i, j: (0, 0)),
                  pl.BlockSpec((1, d), lambda i, j: (0, 0))],
        out_specs=pl.BlockSpec((tt, d), lambda i, j: (i, 0)),
        out_shape=jax.ShapeDtypeStruct((n, d), F32),
        scratch_shapes=[pltpu.VMEM((d, tt), F32), pltpu.VMEM(packed, BF16)],
        compiler_params=_cparams(("parallel", "arbitrary")),
        name="peer",
    )(x1b, x1, u_b, vt_b, cnt, p1, r2, p2, g_row, b_row)


def _stage2(x1, x1b, w_pq, sub_keys, peer_u, peer_v, ln2_g, ln2_b, alpha):
    n, d = x1.shape
    heads, _, nk, half = sub_keys.shape
    tt = min(512, n)
    routing = _route(x1b, w_pq.T.astype(BF16), sub_keys.reshape(heads * 2, nk, half).astype(BF16), min(1024, n))
    return _peer(x1, x1b, peer_u.astype(BF16), peer_v.T.astype(BF16), routing,
                 ln2_g.astype(F32).reshape(1, d), ln2_b.astype(F32).reshape(1, d), alpha, tt, chunk=PEER_CHUNK_I1 * nk)


def kernel(x, w_in, s5_lam_re, s5_lam_im, s5_b_re, s5_b_im, s5_c_re, s5_c_im, s5_d, s5_log_step, s5_w_glu,
           s5_b_glu, hgrn_lb_logits, hgrn_gnorm_w, w_pa, w_pb, w_o, ln1_g, ln1_b, peer_w_q, peer_sub_keys,
           peer_u, peer_v, ln2_g, ln2_b):
    depth = w_in.shape[0]
    alpha = (2.0 * depth) ** 0.25
    lb_all = jnp.cumsum(jax.nn.softmax(hgrn_lb_logits.astype(F32), axis=0), axis=0)[:depth]
    bsz, seq, d = x.shape
    for l in range(depth):
        x1, x1b = _stage1(x, w_in[l], s5_lam_re[l], s5_lam_im[l], s5_b_re[l], s5_b_im[l], s5_c_re[l], s5_c_im[l],
                          s5_d[l], s5_log_step[l], s5_w_glu[l], s5_b_glu[l], lb_all[l], hgrn_gnorm_w[l],
                          w_pa[l], w_pb[l], w_o[l], ln1_g[l], ln1_b[l], alpha)
        x = _stage2(x1, x1b, peer_w_q[l], peer_sub_keys[l], peer_u[l], peer_v[l], ln2_g[l], ln2_b[l],
                    alpha).reshape(bsz, seq, d)
    return x
```

```python
import functools
import math

import numpy as np
import jax
import jax.numpy as jnp
from jax import lax
from jax.experimental import pallas as pl
from jax.experimental.pallas import tpu as pltpu

F32 = jnp.float32
BF16 = jnp.bfloat16

V7X_VMEM_LIMIT_BYTES = 58 * 1024 * 1024
BF16_SUBLANES = 16

S5_GROUPS_PER_BLOCK = 8
HGRN_HEAD_DIM = 128
HGRN_CHUNK = 128
PEER_TOPK = 16
LN_EPS = 1e-5
RMS_EPS = 1e-6


def _cparams(semantics):
    return pltpu.CompilerParams(dimension_semantics=semantics, vmem_limit_bytes=V7X_VMEM_LIMIT_BYTES)


def _dot(a, b):
    return jnp.dot(a, b, preferred_element_type=F32)


def _dot_nt(a, b):
    return lax.dot_general(a, b, (((1,), (1,)), ((), ())), preferred_element_type=F32)


def _dot_tn(a, b):
    return lax.dot_general(a, b, (((0,), (0,)), ((), ())), preferred_element_type=F32)


def _sigmoid(x):
    return 1.0 / (1.0 + jnp.exp(-x))


def _gelu_tanh(x):
    c = math.sqrt(2.0 / math.pi)
    return 0.5 * x * (1.0 + jnp.tanh(c * (x + 0.044715 * (x * x * x))))


def _layer_norm_rows(z, g, b):
    mu = jnp.mean(z, axis=-1, keepdims=True)
    zc = z - mu
    var = jnp.mean(zc * zc, axis=-1, keepdims=True)
    return zc * lax.rsqrt(var + LN_EPS) * g + b


def _proj_kernel(x_ref, w_ref, u_ref, hg_ref, gate_ref, *, s5_w, hg_w, col_chunk):
    xb = x_ref[...].astype(BF16)
    n_cols = w_ref.shape[1]
    for c0 in range(0, n_cols, col_chunk):
        p = _dot(xb, w_ref[:, c0:c0 + col_chunk]).astype(BF16)
        if c0 < s5_w:
            u_ref[:, c0:c0 + col_chunk] = p
        elif c0 < s5_w + hg_w:
            hg_ref[:, c0 - s5_w:c0 - s5_w + col_chunk] = p
        else:
            o = c0 - s5_w - hg_w
            gate_ref[:, o:o + col_chunk] = p


def _proj(x2, w_in_b, bsz, seq, s5_w, hg_w, gate_w, tm):
    n, d = x2.shape
    n_s = seq // tm
    kern = functools.partial(_proj_kernel, s5_w=s5_w, hg_w=hg_w, col_chunk=s5_w)
    return pl.pallas_call(
        kern,
        grid=(bsz, n_s),
        in_specs=[
            pl.BlockSpec((tm, d), lambda b, i: (b * n_s + i, 0)),
            pl.BlockSpec(w_in_b.shape, lambda b, i: (0, 0)),
        ],
        out_specs=[
            pl.BlockSpec((tm, s5_w), lambda b, i: (i, b)),
            pl.BlockSpec((tm, hg_w), lambda b, i: (b * n_s + i, 0)),
            pl.BlockSpec((tm, gate_w), lambda b, i: (b * n_s + i, 0)),
        ],
        out_shape=[
            jax.ShapeDtypeStruct((seq, bsz * s5_w), BF16),
            jax.ShapeDtypeStruct((n, hg_w), BF16),
            jax.ShapeDtypeStruct((n, gate_w), BF16),
        ],
        compiler_params=_cparams(("parallel", "parallel")),
        name="proj",
    )(x2, w_in_b)


def _s5_kernel(u_ref, bblk_ref, cblk_ref, lamr_ref, lami_ref, d_ref, wglu_ref, bglu_ref,
               y_ref, xs_ref, st_ref, *, bsz, tb, n_blk, half):
    @pl.when(pl.program_id(0) == 0)
    def _():
        st_ref[...] = jnp.zeros_like(st_ref)

    in_w = bblk_ref.shape[1]
    ys = []
    for j in range(n_blk):
        uj = u_ref[:, j * in_w:(j + 1) * in_w]
        xs_ref[...] = _dot(uj, bblk_ref[j])
        lr = jnp.broadcast_to(lamr_ref[j], (bsz, half))
        li = jnp.broadcast_to(lami_ref[j], (bsz, half))

        def body(t, carry, lr=lr, li=li):
            sr, si = carry
            rows = pl.ds(pl.multiple_of(t * bsz, bsz), bsz)
            nr = lr * sr - li * si + xs_ref[rows, :half]
            ni = lr * si + li * sr + xs_ref[rows, half:]
            xs_ref[rows, :half] = nr
            xs_ref[rows, half:] = ni
            return nr, ni

        sr, si = lax.fori_loop(0, tb, body, (st_ref[j, :, :half], st_ref[j, :, half:]), unroll=True)
        st_ref[j, :, :half] = sr
        st_ref[j, :, half:] = si
        ys.append(_dot(xs_ref[...].astype(BF16), cblk_ref[j]))
    y = jnp.concatenate(ys, axis=1) + d_ref[...] * u_ref[...].astype(F32)
    y = _gelu_tanh(y)
    z = _dot(y.astype(BF16), wglu_ref[...]) + bglu_ref[...]
    y_ref[...] = (y * _sigmoid(z)).astype(BF16)


def _s5_params(lam_re, lam_im, b_re, b_im, c_re, c_im, d_skip, log_step):
    g, p = lam_re.shape
    h = b_re.shape[-1]
    gb = S5_GROUPS_PER_BLOCK
    nb = g // gb
    lr, li = lam_re.astype(F32), lam_im.astype(F32)
    dt = jnp.exp(log_step.astype(F32))[:, None]
    mag = jnp.exp(lr * dt)
    lbr, lbi = mag * jnp.cos(li * dt), mag * jnp.sin(li * dt)
    den = lr * lr + li * li
    cr = ((lbr - 1.0) * lr + lbi * li) / den
    ci = (lbi * lr - (lbr - 1.0) * li) / den
    br, bi = b_re.astype(F32), b_im.astype(F32)
    bbar_r = cr[..., None] * br - ci[..., None] * bi
    bbar_i = cr[..., None] * bi + ci[..., None] * br
    eye = jnp.eye(gb, dtype=F32)

    def in_blk(m):
        m = m.reshape(nb, gb, p, h)
        return jnp.einsum('jgph,gk->jghkp', m, eye).reshape(nb, gb * h, gb * p)

    def out_blk(m):
        m = m.reshape(nb, gb, h, p)
        return jnp.einsum('jghp,gk->jgpkh', m, eye).reshape(nb, gb * p, gb * h)

    bblk = jnp.concatenate([in_blk(bbar_r), in_blk(bbar_i)], axis=2).astype(BF16)
    cblk = jnp.concatenate([out_blk(c_re.astype(F32)), out_blk(-c_im.astype(F32))], axis=1).astype(BF16)
    lamr = lbr.reshape(nb, 1, gb * p)
    lami = lbi.reshape(nb, 1, gb * p)
    return bblk, cblk, lamr, lami, d_skip.astype(F32).reshape(1, g * h)


def _s5(u_tm, params, w_glu_b, b_glu, bsz, seq, tb):
    bblk, cblk, lamr, lami, d_row = params
    n, w = u_tm.shape
    n_blk, in_w, two_half = bblk.shape
    half = two_half // 2
    rows = tb * bsz
    kern = functools.partial(_s5_kernel, bsz=bsz, tb=tb, n_blk=n_blk, half=half)
    const = lambda a: pl.BlockSpec(a.shape, lambda i: (0,) * a.ndim)
    return pl.pallas_call(
        kern,
        grid=(seq // tb,),
        in_specs=[pl.BlockSpec((rows, w), lambda i: (i, 0)),
                  const(bblk), const(cblk), const(lamr), const(lami), const(d_row),
                  const(w_glu_b), const(b_glu)],
        out_specs=pl.BlockSpec((rows, w), lambda i: (i, 0)),
        out_shape=jax.ShapeDtypeStruct((n, w), BF16),
        scratch_shapes=[pltpu.VMEM((rows, two_half), F32), pltpu.VMEM((n_blk, bsz, two_half), F32)],
        compiler_params=_cparams(("arbitrary",)),
        name="s5",
    )(u_tm, bblk, cblk, lamr, lami, d_row, w_glu_b, b_glu)


def _hgrn_tables(c):
    n_lev = int(math.log2(c))
    t = np.arange(c)[:, None]
    r = np.arange(c)[None, :]
    sel = []
    masks = []
    for lev in range(n_lev):
        w = c >> (lev + 1)
        pos = t % (2 * w)
        a = t - pos + w - 1
        upper = pos >= w
        sel.append(np.where(upper, (r > a) & (r <= t), (r > t) & (r <= a)))
        tt, ss = np.arange(c)[:, None], np.arange(c)[None, :]
        masks.append((tt // (2 * w) == ss // (2 * w)) & (tt % (2 * w) >= w) & (ss % (2 * w) < w))
    sel.append(r <= t)
    sel.append(r > t)
    masks.append(np.eye(c, dtype=bool))
    sel = np.concatenate(sel, axis=0).astype(np.float32)
    sel = np.concatenate([sel, sel], axis=1)
    masks = np.stack(masks, axis=0).astype(np.float32)
    return jnp.asarray(sel, BF16), jnp.asarray(masks, F32)


def _hgrn_kernel(hg_ref, sel_ref, mask_ref, lb_ref, gw_ref, y_ref, st_ref, *, c, n_chunks, width, heads):
    @pl.when(pl.program_id(1) == 0)
    def _():
        st_ref[...] = jnp.zeros_like(st_ref)

    n_lev = mask_ref.shape[0] - 1
    dh = width // heads
    lb = lb_ref[...]
    gw = gw_ref[...]

    def chunk(ci, carry):
        rows = pl.ds(pl.multiple_of(ci * c, c), c)
        q = hg_ref[0, rows, 0:width].astype(F32)
        fp = hg_ref[0, rows, width:2 * width].astype(F32)
        v = hg_ref[0, rows, 2 * width:3 * width]
        g = hg_ref[0, rows, 3 * width:4 * width].astype(F32)
        f = lb + (1.0 - lb) * _sigmoid(fp)
        lf = jnp.log(f)
        k = 1.0 - f
        qf = q * _sigmoid(q)
        h1 = lf.astype(BF16)
        h2 = (lf - h1.astype(F32)).astype(BF16)
        e_all = jnp.exp(_dot(sel_ref[...], jnp.concatenate([h1, h2], axis=0)))
        e_cum = e_all[n_lev * c:(n_lev + 1) * c]
        e_suf = e_all[(n_lev + 1) * c:(n_lev + 2) * c]
        outs = []
        for hd in range(heads):
            cols = slice(hd * dh, (hd + 1) * dh)
            qh, kh = qf[:, cols], k[:, cols]
            vh = v[:, cols]
            sc = mask_ref[n_lev] * _dot_nt(qh.astype(BF16), kh.astype(BF16))
            for lev in range(n_lev):
                el = e_all[lev * c:(lev + 1) * c, cols]
                sc = sc + mask_ref[lev] * _dot_nt((qh * el).astype(BF16), (kh * el).astype(BF16))
            st = st_ref[hd]
            o = _dot_nt((qh * e_cum[:, cols]).astype(BF16), st.astype(BF16)) + _dot(sc.astype(BF16), vh)
            e_tot = e_cum[c - 1:c, cols]
            st_ref[hd] = st * e_tot + _dot_tn(vh, (kh * e_suf[:, cols]).astype(BF16))
            o = o * lax.rsqrt(jnp.mean(o * o, axis=-1, keepdims=True) + RMS_EPS)
            outs.append(o)
        o = jnp.concatenate(outs, axis=1) * gw * (g * _sigmoid(g))
        y_ref[0, rows, :] = o.astype(BF16)
        return carry

    lax.fori_loop(0, n_chunks, chunk, 0, unroll=True)


def _hgrn(hg3, lb_row, gw_row, heads, tbh):
    bsz, seq, w4 = hg3.shape
    width = w4 // 4
    c = HGRN_CHUNK
    sel, masks = _hgrn_tables(c)
    kern = functools.partial(_hgrn_kernel, c=c, n_chunks=tbh // c, width=width, heads=heads)
    const = lambda a: pl.BlockSpec(a.shape, lambda b, i: (0,) * a.ndim)
    return pl.pallas_call(
        kern,
        grid=(bsz, seq // tbh),
        in_specs=[pl.BlockSpec((1, tbh, w4), lambda b, i: (b, i, 0)),
                  const(sel), const(masks), const(lb_row), const(gw_row)],
        out_specs=pl.BlockSpec((1, tbh, width), lambda b, i: (b, i, 0)),
        out_shape=jax.ShapeDtypeStruct((bsz, seq, width), BF16),
        scratch_shapes=[pltpu.VMEM((heads, width // heads, width // heads), F32)],
        compiler_params=_cparams(("parallel", "arbitrary")),
        name="hgrn",
    )(hg3, sel, masks, lb_row, gw_row)


def _merge_kernel(x_ref, ya_ref, yb_ref, gate_ref, wpa_ref, wpb_ref, wo_ref, g_ref, b_ref,
                  x1_ref, x1b_ref, *, alpha, d):
    ga = gate_ref[:, :d].astype(F32)
    gb = gate_ref[:, d:].astype(F32)
    merged = _sigmoid(ga) * _dot(ya_ref[...], wpa_ref[...]) + _sigmoid(gb) * _dot(yb_ref[...], wpb_ref[...])
    mix = _dot(merged.astype(BF16), wo_ref[...])
    x1 = _layer_norm_rows(alpha * x_ref[...] + mix, g_ref[...], b_ref[...])
    x1_ref[...] = x1
    x1b_ref[...] = x1.astype(BF16)


def _merge(x2, ya_tm, yb2, gates, wpa_b, wpb_b, wo_b, g_row, b_row, bsz, seq, alpha, tm):
    n, d = x2.shape
    w = yb2.shape[1]
    n_s = seq // tm
    kern = functools.partial(_merge_kernel, alpha=alpha, d=d)
    const = lambda a: pl.BlockSpec(a.shape, lambda b, i: (0,) * a.ndim)
    row = lambda cols: pl.BlockSpec((tm, cols), lambda b, i: (b * n_s + i, 0))
    return pl.pallas_call(
        kern,
        grid=(bsz, n_s),
        in_specs=[row(d),
                  pl.BlockSpec((tm, w), lambda b, i: (i, b)),
                  row(w), row(2 * d),
                  const(wpa_b), const(wpb_b), const(wo_b), const(g_row), const(b_row)],
        out_specs=[row(d), row(d)],
        out_shape=[jax.ShapeDtypeStruct((n, d), F32), jax.ShapeDtypeStruct((n, d), BF16)],
        compiler_params=_cparams(("parallel", "parallel")),
        name="merge",
    )(x2, ya_tm, yb2, gates, wpa_b, wpb_b, wo_b, g_row, b_row)


def _stage1(x, w_in, s5_lam_re, s5_lam_im, s5_b_re, s5_b_im, s5_c_re, s5_c_im, s5_d, s5_log_step,
            s5_w_glu, s5_b_glu, lb, gnorm_w, w_pa, w_pb, w_o, ln1_g, ln1_b, alpha):
    bsz, seq, d = x.shape
    n = bsz * seq
    s5_w = s5_w_glu.shape[0]
    hg_w = 4 * gnorm_w.shape[0]
    gate_w = 2 * d
    tm = min(512, seq)
    x2 = x.reshape(n, d)
    u_tm, hg, gates = _proj(x2, w_in.astype(BF16), bsz, seq, s5_w, hg_w, gate_w, tm)
    s5p = _s5_params(s5_lam_re, s5_lam_im, s5_b_re, s5_b_im, s5_c_re, s5_c_im, s5_d, s5_log_step)
    ya = _s5(u_tm.reshape(seq * bsz, s5_w), s5p, s5_w_glu.astype(BF16),
             s5_b_glu.astype(F32).reshape(1, s5_w), bsz, seq, tb=min(128, seq))
    yb = _hgrn(hg.reshape(bsz, seq, hg_w), lb.reshape(1, -1), gnorm_w.astype(F32).reshape(1, -1),
               heads=gnorm_w.shape[0] // HGRN_HEAD_DIM, tbh=min(1024, seq))
    return _merge(x2, ya.reshape(seq, bsz * s5_w), yb.reshape(n, -1), gates,
                  w_pa.astype(BF16), w_pb.astype(BF16), w_o.astype(BF16),
                  ln1_g.astype(F32).reshape(1, d), ln1_b.astype(F32).reshape(1, d), bsz, seq, alpha, tm)


RANK_BASE = 2.0 ** 100


def _top_sorted(s, k):
    rows = []
    cur = s
    for j in range(k):
        mx = jnp.max(cur, axis=0, keepdims=True)
        rows.append(mx)
        cur = jnp.where(cur == mx, -(j + 1) * RANK_BASE, cur)
    rank = jnp.where(cur < -0.5 * RANK_BASE, cur * (-1.0 / RANK_BASE) - 1.0, float(k))
    return rows, rank


def _route_kernel(xb_ref, wq_ref, keys_ref, cnt_ref, p1_ref, r2_ref, p2_ref, s_scr, *, heads, topk):
    nk = keys_ref.shape[1]

    q_t = _dot_nt(wq_ref[...], xb_ref[...])
    for hp in range(2 * heads):
        s_scr[hp] = _dot(keys_ref[hp], q_t[hp * nk:(hp + 1) * nk, :].astype(BF16))

    def head(h, carry):
        s1 = s_scr[2 * h]
        s2 = s_scr[2 * h + 1]
        a, rank1 = _top_sorted(s1, topk)
        b, rank2 = _top_sorted(s2, topk)
        a_all = jnp.concatenate(a, axis=0)
        b_all = jnp.concatenate(b, axis=0)
        half_rows = topk // 2
        j_idx = lax.broadcasted_iota(jnp.int32, (half_rows, 1), 0)
        cands = [a_all + b[0]]
        for l in range(1, half_rows):
            cands.append(jnp.where(j_idx < topk // (l + 1), a_all[:half_rows] + b[l], -jnp.inf))
        cands.append(a[0] + b_all[half_rows:])
        cur = jnp.concatenate(cands, axis=0)
        m = a[0] + b[0]
        z = jnp.zeros_like(m)
        tau = m
        for r in range(topk):
            tau = jnp.max(cur, axis=0, keepdims=True)
            z = z + jnp.exp(tau - m)
            if r + 1 < topk:
                cur = jnp.where(cur == tau, -jnp.inf, cur)
        hit = lambda c: jnp.where(c >= tau, 1.0, 0.0)
        cnt_lo = hit(cands[0][:half_rows])
        for l in range(1, half_rows):
            cnt_lo = cnt_lo + hit(cands[l])
        cnt_0 = cnt_lo[0:1] + jnp.sum(hit(cands[half_rows]), axis=0, keepdims=True)
        cnt_rows = jnp.concatenate([cnt_lo, hit(cands[0][half_rows:])], axis=0)
        cnt = jnp.zeros(s1.shape, F32)
        for c in range(half_rows):
            ranks_with = jnp.sum(jnp.where(cnt_rows > float(c), 1.0, 0.0), axis=0, keepdims=True)
            cnt = jnp.where(rank1 < ranks_with, float(c + 1), cnt)
        cnt = jnp.where(rank1 == 0.0, jnp.maximum(cnt, cnt_0), cnt)
        cnt_ref[h] = cnt.astype(BF16)
        p1_ref[h] = jnp.exp(s1 - a[0]).astype(BF16)
        r2_ref[h] = rank2.astype(BF16)
        p2_ref[h] = (jnp.exp(s2 - b[0]) / z).astype(BF16)
        return carry

    lax.fori_loop(0, heads, head, 0)


def _route(x1b, wq_t, keys, tt):
    n, d = x1b.shape
    hp, nk, _ = keys.shape
    heads = hp // 2
    kern = functools.partial(_route_kernel, heads=heads, topk=PEER_TOPK)
    out = lambda dt: jax.ShapeDtypeStruct((heads, nk, n), dt)
    ospec = pl.BlockSpec((heads, nk, tt), lambda i: (0, 0, i))
    return pl.pallas_call(
        kern,
        grid=(n // tt,),
        in_specs=[pl.BlockSpec((tt, d), lambda i: (i, 0)),
                  pl.BlockSpec(wq_t.shape, lambda i: (0, 0)),
                  pl.BlockSpec(keys.shape, lambda i: (0, 0, 0))],
        out_specs=[ospec, ospec, ospec, ospec],
        out_shape=[out(BF16), out(BF16), out(BF16), out(BF16)],
        scratch_shapes=[pltpu.VMEM((hp, nk, tt), F32)],
        compiler_params=_cparams(("parallel",)),
        name="route",
    )(x1b, wq_t, keys)


PEER_DOT_SPLITS = (2, 2) + (4,) * 7
PEER_CHUNK_I1 = sum(PEER_DOT_SPLITS)


def _peer_kernel(xb_ref, x_ref, u_ref, vt_ref, cnt_ref, p1_ref, r2_ref, p2_ref, g_ref, b_ref,
                 out_ref, acc_ref, h_ref, *, heads, nk, alpha):
    j = pl.program_id(1)
    tt = xb_ref.shape[0]
    n_i1 = u_ref.shape[0] // nk
    pk = BF16_SUBLANES
    tiles = nk // pk

    @pl.when(j == 0)
    def _():
        acc_ref[...] = jnp.zeros_like(acc_ref)

    cnt_t = [cnt_ref[h].astype(F32) for h in range(heads)]
    p1_t = [p1_ref[h].astype(F32) for h in range(heads)]

    def expert_dots(i0, n):
        h = _dot_nt(u_ref[i0 * nk:(i0 + n) * nk, :], xb_ref[...])
        h_ref[i0 * tiles:(i0 + n) * tiles] = h.astype(BF16).reshape(n * tiles, pk, tt)

    def gate_slab(ii):
        w = jnp.zeros((tiles, pk, tt), BF16)
        for h in range(heads):
            cnt = jnp.broadcast_to(cnt_t[h][ii:ii + 1], (pk, tt)).astype(BF16)
            p1 = jnp.broadcast_to(p1_t[h][ii:ii + 1], (pk, tt)).astype(BF16)
            r2 = r2_ref[h].reshape(tiles, pk, tt)
            p2 = p2_ref[h].reshape(tiles, pk, tt)
            w = w + jnp.where(r2 < cnt[None], p2, 0.0) * p1[None]
        h_ref[ii * tiles:(ii + 1) * tiles] = w * _gelu_tanh(h_ref[ii * tiles:(ii + 1) * tiles])

    starts = [sum(PEER_DOT_SPLITS[:k]) for k in range(len(PEER_DOT_SPLITS))]
    expert_dots(0, PEER_DOT_SPLITS[0])
    for k, n in enumerate(PEER_DOT_SPLITS):
        if k + 1 < len(PEER_DOT_SPLITS):
            expert_dots(starts[k + 1], PEER_DOT_SPLITS[k + 1])
        for ii in range(starts[k], starts[k] + n):
            gate_slab(ii)
    acc_ref[...] += _dot(vt_ref[...], h_ref[...].reshape(n_i1 * nk, tt))

    @pl.when(j == pl.num_programs(1) - 1)
    def _():
        z = alpha * x_ref[...] + acc_ref[...].T
        out_ref[...] = _layer_norm_rows(z, g_ref[...], b_ref[...])


def _peer(x1, x1b, u_b, vt_b, routing, g_row, b_row, alpha, tt, chunk):
    n, d = x1.shape
    cnt, p1, r2, p2 = routing
    heads, nk, _ = cnt.shape
    n_exp = u_b.shape[0]
    assert chunk == PEER_CHUNK_I1 * nk and PEER_CHUNK_I1 % BF16_SUBLANES == 0, "whole packed tiles of i1 rows per step"
    kern = functools.partial(_peer_kernel, heads=heads, nk=nk, alpha=alpha)
    rspec = pl.BlockSpec((heads, nk, tt), lambda i, j: (0, 0, i))
    cspec = pl.BlockSpec((heads, PEER_CHUNK_I1, tt), lambda i, j: (0, j, i))
    packed = (chunk // BF16_SUBLANES, BF16_SUBLANES, tt)
    return pl.pallas_call(
        kern,
        grid=(n // tt, n_exp // chunk),
        in_specs=[pl.BlockSpec((tt, d), lambda i, j: (i, 0)),
                  pl.BlockSpec((tt, d), lambda i, j: (i, 0)),
                  pl.BlockSpec((chunk, d), lambda i, j: (j, 0)),
                  pl.BlockSpec((d, chunk), lambda i, j: (0, j)),
                  cspec, cspec, rspec, rspec,
                  pl.BlockSpec((1, d), lambda i, j: (0, 0)),
                  pl.BlockSpec((1, d), lambda i, j: (0, 0))],
        out_specs=pl.BlockSpec((tt, d), lambda i, j: (i, 0)),
        out_shape=jax.ShapeDtypeStruct((n, d), F32),
        scratch_shapes=[pltpu.VMEM((d, tt), F32), pltpu.VMEM(packed, BF16)],
        compiler_params=_cparams(("parallel", "arbitrary")),
        name="peer",
    )(x1b, x1, u_b, vt_b, cnt, p1, r2, p2, g_row, b_row)


def _stage2(x1, x1b, w_pq, sub_keys, peer_u, peer_v, ln2_g, ln2_b, alpha):
    n, d = x1.shape
    heads, _, nk, half = sub_keys.shape
    tt = min(512, n)
    routing = _route(x1b, w_pq.T.astype(BF16), sub_keys.reshape(heads * 2, nk, half).astype(BF16), min(1024, n))
    return _peer(x1, x1b, peer_u.astype(BF16), peer_v.T.astype(BF16), routing,
                 ln2_g.astype(F32).reshape(1, d), ln2_b.astype(F32).reshape(1, d), alpha, tt, chunk=PEER_CHUNK_I1 * nk)


def kernel(x, w_in, s5_lam_re, s5_lam_im, s5_b_re, s5_b_im, s5_c_re, s5_c_im, s5_d, s5_log_step, s5_w_glu,
           s5_b_glu, hgrn_lb_logits, hgrn_gnorm_w, w_pa, w_pb, w_o, ln1_g, ln1_b, peer_w_q, peer_sub_keys,
           peer_u, peer_v, ln2_g, ln2_b):
    depth = w_in.shape[0]
    alpha = (2.0 * depth) ** 0.25
    lb_all = jnp.cumsum(jax.nn.softmax(hgrn_lb_logits.astype(F32), axis=0), axis=0)[:depth]
    bsz, seq, d = x.shape
    for l in range(depth):
        x1, x1b = _stage1(x, w_in[l], s5_lam_re[l], s5_lam_im[l], s5_b_re[l], s5_b_im[l], s5_c_re[l], s5_c_im[l],
                          s5_d[l], s5_log_step[l], s5_w_glu[l], s5_b_glu[l], lb_all[l], hgrn_gnorm_w[l],
                          w_pa[l], w_pb[l], w_o[l], ln1_g[l], ln1_b[l], alpha)
        x = _stage2(x1, x1b, peer_w_q[l], peer_sub_keys[l], peer_u[l], peer_v[l], ln2_g[l], ln2_b[l],
                    alpha).reshape(bsz, seq, d)
    return x
```

```python
import functools
import math

import numpy as np
import jax
import jax.numpy as jnp
from jax import lax
from jax.experimental import pallas as pl
from jax.experimental.pallas import tpu as pltpu

F32 = jnp.float32
BF16 = jnp.bfloat16

V7X_VMEM_LIMIT_BYTES = 58 * 1024 * 1024
BF16_SUBLANES = 16

S5_GROUPS_PER_BLOCK = 8
HGRN_HEAD_DIM = 128
HGRN_CHUNK = 128
PEER_TOPK = 16
LN_EPS = 1e-5
RMS_EPS = 1e-6


def _cparams(semantics):
    return pltpu.CompilerParams(dimension_semantics=semantics, vmem_limit_bytes=V7X_VMEM_LIMIT_BYTES)


def _dot(a, b):
    return jnp.dot(a, b, preferred_element_type=F32)


def _dot_nt(a, b):
    return lax.dot_general(a, b, (((1,), (1,)), ((), ())), preferred_element_type=F32)


def _dot_tn(a, b):
    return lax.dot_general(a, b, (((0,), (0,)), ((), ())), preferred_element_type=F32)


def _sigmoid(x):
    return 1.0 / (1.0 + jnp.exp(-x))


def _gelu_tanh(x):
    c = math.sqrt(2.0 / math.pi)
    return 0.5 * x * (1.0 + jnp.tanh(c * (x + 0.044715 * (x * x * x))))


def _layer_norm_rows(z, g, b):
    mu = jnp.mean(z, axis=-1, keepdims=True)
    zc = z - mu
    var = jnp.mean(zc * zc, axis=-1, keepdims=True)
    return zc * lax.rsqrt(var + LN_EPS) * g + b


def _proj_kernel(x_ref, w_ref, u_ref, hg_ref, gate_ref, *, s5_w, hg_w, col_chunk):
    xb = x_ref[...].astype(BF16)
    n_cols = w_ref.shape[1]
    for c0 in range(0, n_cols, col_chunk):
        p = _dot(xb, w_ref[:, c0:c0 + col_chunk]).astype(BF16)
        if c0 < s5_w:
            u_ref[:, c0:c0 + col_chunk] = p
        elif c0 < s5_w + hg_w:
            hg_ref[:, c0 - s5_w:c0 - s5_w + col_chunk] = p
        else:
            o = c0 - s5_w - hg_w
            gate_ref[:, o:o + col_chunk] = p


def _proj(x2, w_in_b, bsz, seq, s5_w, hg_w, gate_w, tm):
    n, d = x2.shape
    n_s = seq // tm
    kern = functools.partial(_proj_kernel, s5_w=s5_w, hg_w=hg_w, col_chunk=s5_w)
    return pl.pallas_call(
        kern,
        grid=(bsz, n_s),
        in_specs=[
            pl.BlockSpec((tm, d), lambda b, i: (b * n_s + i, 0)),
            pl.BlockSpec(w_in_b.shape, lambda b, i: (0, 0)),
        ],
        out_specs=[
            pl.BlockSpec((tm, s5_w), lambda b, i: (i, b)),
            pl.BlockSpec((tm, hg_w), lambda b, i: (b * n_s + i, 0)),
            pl.BlockSpec((tm, gate_w), lambda b, i: (b * n_s + i, 0)),
        ],
        out_shape=[
            jax.ShapeDtypeStruct((seq, bsz * s5_w), BF16),
            jax.ShapeDtypeStruct((n, hg_w), BF16),
            jax.ShapeDtypeStruct((n, gate_w), BF16),
        ],
        compiler_params=_cparams(("parallel", "parallel")),
        name="proj",
    )(x2, w_in_b)


def _s5_kernel(u_ref, bblk_ref, cblk_ref, lamr_ref, lami_ref, d_ref, wglu_ref, bglu_ref,
               y_ref, xs_ref, st_ref, *, bsz, tb, n_blk, half):
    @pl.when(pl.program_id(0) == 0)
    def _():
        st_ref[...] = jnp.zeros_like(st_ref)

    in_w = bblk_ref.shape[1]
    ys = []
    for j in range(n_blk):
        uj = u_ref[:, j * in_w:(j + 1) * in_w]
        xs_ref[...] = _dot(uj, bblk_ref[j])
        lr = jnp.broadcast_to(lamr_ref[j], (bsz, half))
        li = jnp.broadcast_to(lami_ref[j], (bsz, half))

        def body(t, carry, lr=lr, li=li):
            sr, si = carry
            rows = pl.ds(pl.multiple_of(t * bsz, bsz), bsz)
            nr = lr * sr - li * si + xs_ref[rows, :half]
            ni = lr * si + li * sr + xs_ref[rows, half:]
            xs_ref[rows, :half] = nr
            xs_ref[rows, half:] = ni
            return nr, ni

        sr, si = lax.fori_loop(0, tb, body, (st_ref[j, :, :half], st_ref[j, :, half:]), unroll=True)
        st_ref[j, :, :half] = sr
        st_ref[j, :, half:] = si
        ys.append(_dot(xs_ref[...].astype(BF16), cblk_ref[j]))
    y = jnp.concatenate(ys, axis=1) + d_ref[...] * u_ref[...].astype(F32)
    y = _gelu_tanh(y)
    z = _dot(y.astype(BF16), wglu_ref[...]) + bglu_ref[...]
    y_ref[...] = (y * _sigmoid(z)).astype(BF16)


def _s5_params(lam_re, lam_im, b_re, b_im, c_re, c_im, d_skip, log_step):
    g, p = lam_re.shape
    h = b_re.shape[-1]
    gb = S5_GROUPS_PER_BLOCK
    nb = g // gb
    lr, li = lam_re.astype(F32), lam_im.astype(F32)
    dt = jnp.exp(log_step.astype(F32))[:, None]
    mag = jnp.exp(lr * dt)
    lbr, lbi = mag * jnp.cos(li * dt), mag * jnp.sin(li * dt)
    den = lr * lr + li * li
    cr = ((lbr - 1.0) * lr + lbi * li) / den
    ci = (lbi * lr - (lbr - 1.0) * li) / den
    br, bi = b_re.astype(F32), b_im.astype(F32)
    bbar_r = cr[..., None] * br - ci[..., None] * bi
    bbar_i = cr[..., None] * bi + ci[..., None] * br
    eye = jnp.eye(gb, dtype=F32)

    def in_blk(m):
        m = m.reshape(nb, gb, p, h)
        return jnp.einsum('jgph,gk->jghkp', m, eye).reshape(nb, gb * h, gb * p)

    def out_blk(m):
        m = m.reshape(nb, gb, h, p)
        return jnp.einsum('jghp,gk->jgpkh', m, eye).reshape(nb, gb * p, gb * h)

    bblk = jnp.concatenate([in_blk(bbar_r), in_blk(bbar_i)], axis=2).astype(BF16)
    cblk = jnp.concatenate([out_blk(c_re.astype(F32)), out_blk(-c_im.astype(F32))], axis=1).astype(BF16)
    lamr = lbr.reshape(nb, 1, gb * p)
    lami = lbi.reshape(nb, 1, gb * p)
    return bblk, cblk, lamr, lami, d_skip.astype(F32).reshape(1, g * h)


def _s5(u_tm, params, w_glu_b, b_glu, bsz, seq, tb):
    bblk, cblk, lamr, lami, d_row = params
    n, w = u_tm.shape
    n_blk, in_w, two_half = bblk.shape
    half = two_half // 2
    rows = tb * bsz
    kern = functools.partial(_s5_kernel, bsz=bsz, tb=tb, n_blk=n_blk, half=half)
    const = lambda a: pl.BlockSpec(a.shape, lambda i: (0,) * a.ndim)
    return pl.pallas_call(
        kern,
        grid=(seq // tb,),
        in_specs=[pl.BlockSpec((rows, w), lambda i: (i, 0)),
                  const(bblk), const(cblk), const(lamr), const(lami), const(d_row),
                  const(w_glu_b), const(b_glu)],
        out_specs=pl.BlockSpec((rows, w), lambda i: (i, 0)),
        out_shape=jax.ShapeDtypeStruct((n, w), BF16),
        scratch_shapes=[pltpu.VMEM((rows, two_half), F32), pltpu.VMEM((n_blk, bsz, two_half), F32)],
        compiler_params=_cparams(("arbitrary",)),
        name="s5",
    )(u_tm, bblk, cblk, lamr, lami, d_row, w_glu_b, b_glu)


def _hgrn_tables(c):
    n_lev = int(math.log2(c))
    t = np.arange(c)[:, None]
    r = np.arange(c)[None, :]
    sel = []
    masks = []
    for lev in range(n_lev):
        w = c >> (lev + 1)
        pos = t % (2 * w)
        a = t - pos + w - 1
        upper = pos >= w
        sel.append(np.where(upper, (r > a) & (r <= t), (r > t) & (r <= a)))
        tt, ss = np.arange(c)[:, None], np.arange(c)[None, :]
        masks.append((tt // (2 * w) == ss // (2 * w)) & (tt % (2 * w) >= w) & (ss % (2 * w) < w))
    sel.append(r <= t)
    sel.append(r > t)
    masks.append(np.eye(c, dtype=bool))
    sel = np.concatenate(sel, axis=0).astype(np.float32)
    sel = np.concatenate([sel, sel], axis=1)
    masks = np.stack(masks, axis=0).astype(np.float32)
    return jnp.asarray(sel, BF16), jnp.asarray(masks, F32)


def _hgrn_kernel(hg_ref, sel_ref, mask_ref, lb_ref, gw_ref, y_ref, st_ref, *, c, n_chunks, width, heads):
    @pl.when(pl.program_id(1) == 0)
    def _():
        st_ref[...] = jnp.zeros_like(st_ref)

    n_lev = mask_ref.shape[0] - 1
    dh = width // heads
    lb = lb_ref[...]
    gw = gw_ref[...]

    def chunk(ci, carry):
        rows = pl.ds(pl.multiple_of(ci * c, c), c)
        q = hg_ref[0, rows, 0:width].astype(F32)
        fp = hg_ref[0, rows, width:2 * width].astype(F32)
        v = hg_ref[0, rows, 2 * width:3 * width]
        g = hg_ref[0, rows, 3 * width:4 * width].astype(F32)
        f = lb + (1.0 - lb) * _sigmoid(fp)
        lf = jnp.log(f)
        k = 1.0 - f
        qf = q * _sigmoid(q)
        h1 = lf.astype(BF16)
        h2 = (lf - h1.astype(F32)).astype(BF16)
        e_all = jnp.exp(_dot(sel_ref[...], jnp.concatenate([h1, h2], axis=0)))
        e_cum = e_all[n_lev * c:(n_lev + 1) * c]
        e_suf = e_all[(n_lev + 1) * c:(n_lev + 2) * c]
        outs = []
        for hd in range(heads):
            cols = slice(hd * dh, (hd + 1) * dh)
            qh, kh = qf[:, cols], k[:, cols]
            vh = v[:, cols]
            sc = mask_ref[n_lev] * _dot_nt(qh.astype(BF16), kh.astype(BF16))
            for lev in range(n_lev):
                el = e_all[lev * c:(lev + 1) * c, cols]
                sc = sc + mask_ref[lev] * _dot_nt((qh * el).astype(BF16), (kh * el).astype(BF16))
            st = st_ref[hd]
            o = _dot_nt((qh * e_cum[:, cols]).astype(BF16), st.astype(BF16)) + _dot(sc.astype(BF16), vh)
            e_tot = e_cum[c - 1:c, cols]
            st_ref[hd] = st * e_tot + _dot_tn(vh, (kh * e_suf[:, cols]).astype(BF16))
            o = o * lax.rsqrt(jnp.mean(o * o, axis=-1, keepdims=True) + RMS_EPS)
            outs.append(o)
        o = jnp.concatenate(outs, axis=1) * gw * (g * _sigmoid(g))
        y_ref[0, rows, :] = o.astype(BF16)
        return carry

    lax.fori_loop(0, n_chunks, chunk, 0, unroll=True)


def _hgrn(hg3, lb_row, gw_row, heads, tbh):
    bsz, seq, w4 = hg3.shape
    width = w4 // 4
    c = HGRN_CHUNK
    sel, masks = _hgrn_tables(c)
    kern = functools.partial(_hgrn_kernel, c=c, n_chunks=tbh // c, width=width, heads=heads)
    const = lambda a: pl.BlockSpec(a.shape, lambda b, i: (0,) * a.ndim)
    return pl.pallas_call(
        kern,
        grid=(bsz, seq // tbh),
        in_specs=[pl.BlockSpec((1, tbh, w4), lambda b, i: (b, i, 0)),
                  const(sel), const(masks), const(lb_row), const(gw_row)],
        out_specs=pl.BlockSpec((1, tbh, width), lambda b, i: (b, i, 0)),
        out_shape=jax.ShapeDtypeStruct((bsz, seq, width), BF16),
        scratch_shapes=[pltpu.VMEM((heads, width // heads, width // heads), F32)],
        compiler_params=_cparams(("parallel", "arbitrary")),
        name="hgrn",
    )(hg3, sel, masks, lb_row, gw_row)


def _merge_kernel(x_ref, ya_ref, yb_ref, gate_ref, wpa_ref, wpb_ref, wo_ref, g_ref, b_ref,
                  x1_ref, x1b_ref, *, alpha, d):
    ga = gate_ref[:, :d].astype(F32)
    gb = gate_ref[:, d:].astype(F32)
    merged = _sigmoid(ga) * _dot(ya_ref[...], wpa_ref[...]) + _sigmoid(gb) * _dot(yb_ref[...], wpb_ref[...])
    mix = _dot(merged.astype(BF16), wo_ref[...])
    x1 = _layer_norm_rows(alpha * x_ref[...] + mix, g_ref[...], b_ref[...])
    x1_ref[...] = x1
    x1b_ref[...] = x1.astype(BF16)


def _merge(x2, ya_tm, yb2, gates, wpa_b, wpb_b, wo_b, g_row, b_row, bsz, seq, alpha, tm):
    n, d = x2.shape
    w = yb2.shape[1]
    n_s = seq // tm
    kern = functools.partial(_merge_kernel, alpha=alpha, d=d)
    const = lambda a: pl.BlockSpec(a.shape, lambda b, i: (0,) * a.ndim)
    row = lambda cols: pl.BlockSpec((tm, cols), lambda b, i: (b * n_s + i, 0))
    return pl.pallas_call(
        kern,
        grid=(bsz, n_s),
        in_specs=[row(d),
                  pl.BlockSpec((tm, w), lambda b, i: (i, b)),
                  row(w), row(2 * d),
                  const(wpa_b), const(wpb_b), const(wo_b), const(g_row), const(b_row)],
        out_specs=[row(d), row(d)],
        out_shape=[jax.ShapeDtypeStruct((n, d), F32), jax.ShapeDtypeStruct((n, d), BF16)],
        compiler_params=_cparams(("parallel", "parallel")),
        name="merge",
    )(x2, ya_tm, yb2, gates, wpa_b, wpb_b, wo_b, g_row, b_row)


def _stage1(x, w_in, s5_lam_re, s5_lam_im, s5_b_re, s5_b_im, s5_c_re, s5_c_im, s5_d, s5_log_step,
            s5_w_glu, s5_b_glu, lb, gnorm_w, w_pa, w_pb, w_o, ln1_g, ln1_b, alpha):
    bsz, seq, d = x.shape
    n = bsz * seq
    s5_w = s5_w_glu.shape[0]
    hg_w = 4 * gnorm_w.shape[0]
    gate_w = 2 * d
    tm = min(512, seq)
    x2 = x.reshape(n, d)
    u_tm, hg, gates = _proj(x2, w_in.astype(BF16), bsz, seq, s5_w, hg_w, gate_w, tm)
    s5p = _s5_params(s5_lam_re, s5_lam_im, s5_b_re, s5_b_im, s5_c_re, s5_c_im, s5_d, s5_log_step)
    ya = _s5(u_tm.reshape(seq * bsz, s5_w), s5p, s5_w_glu.astype(BF16),
             s5_b_glu.astype(F32).reshape(1, s5_w), bsz, seq, tb=min(128, seq))
    yb = _hgrn(hg.reshape(bsz, seq, hg_w), lb.reshape(1, -1), gnorm_w.astype(F32).reshape(1, -1),
               heads=gnorm_w.shape[0] // HGRN_HEAD_DIM, tbh=min(1024, seq))
    return _merge(x2, ya.reshape(seq, bsz * s5_w), yb.reshape(n, -1), gates,
                  w_pa.astype(BF16), w_pb.astype(BF16), w_o.astype(BF16),
                  ln1_g.astype(F32).reshape(1, d), ln1_b.astype(F32).reshape(1, d), bsz, seq, alpha, tm)


RANK_BASE = 2.0 ** 100


def _top_sorted(s, k):
    rows = []
    cur = s
    for j in range(k):
        mx = jnp.max(cur, axis=0, keepdims=True)
        rows.append(mx)
        cur = jnp.where(cur == mx, -(j + 1) * RANK_BASE, cur)
    rank = jnp.where(cur < -0.5 * RANK_BASE, cur * (-1.0 / RANK_BASE) - 1.0, float(k))
    return rows, rank


def _route_kernel(xb_ref, wk_ref, cnt_ref, p1_ref, r2_ref, p2_ref, s_scr, *, heads, topk):
    s_scr[...] = _dot_nt(wk_ref[...], xb_ref[...]).reshape(s_scr.shape)

    def head(h, carry):
        s1 = s_scr[2 * h]
        s2 = s_scr[2 * h + 1]
        a, rank1 = _top_sorted(s1, topk)
        b, rank2 = _top_sorted(s2, topk)
        a_all = jnp.concatenate(a, axis=0)
        b_all = jnp.concatenate(b, axis=0)
        half_rows = topk // 2
        j_idx = lax.broadcasted_iota(jnp.int32, (half_rows, 1), 0)
        cands = [a_all + b[0]]
        for l in range(1, half_rows):
            cands.append(jnp.where(j_idx < topk // (l + 1), a_all[:half_rows] + b[l], -jnp.inf))
        cands.append(a[0] + b_all[half_rows:])
        cur = jnp.concatenate(cands, axis=0)
        m = a[0] + b[0]
        z = jnp.zeros_like(m)
        tau = m
        for r in range(topk):
            tau = jnp.max(cur, axis=0, keepdims=True)
            z = z + jnp.exp(tau - m)
            if r + 1 < topk:
                cur = jnp.where(cur == tau, -jnp.inf, cur)
        hit = lambda c: jnp.where(c >= tau, 1.0, 0.0)
        cnt_lo = hit(cands[0][:half_rows])
        for l in range(1, half_rows):
            cnt_lo = cnt_lo + hit(cands[l])
        cnt_0 = cnt_lo[0:1] + jnp.sum(hit(cands[half_rows]), axis=0, keepdims=True)
        cnt_rows = jnp.concatenate([cnt_lo, hit(cands[0][half_rows:])], axis=0)
        cnt = jnp.zeros(s1.shape, F32)
        for c in range(half_rows):
            ranks_with = jnp.sum(jnp.where(cnt_rows > float(c), 1.0, 0.0), axis=0, keepdims=True)
            cnt = jnp.where(rank1 < ranks_with, float(c + 1), cnt)
        cnt = jnp.where(rank1 == 0.0, jnp.maximum(cnt, cnt_0), cnt)
        cnt_ref[h] = cnt.astype(BF16)
        p1_ref[h] = jnp.exp(s1 - a[0]).astype(BF16)
        r2_ref[h] = rank2.astype(BF16)
        p2_ref[h] = (jnp.exp(s2 - b[0]) / z).astype(BF16)
        return carry

    lax.fori_loop(0, heads, head, 0)


def _route(x1b, wk_t, heads, nk, tt):
    n, d = x1b.shape
    hp = 2 * heads
    kern = functools.partial(_route_kernel, heads=heads, topk=PEER_TOPK)
    out = lambda dt: jax.ShapeDtypeStruct((heads, nk, n), dt)
    ospec = pl.BlockSpec((heads, nk, tt), lambda i: (0, 0, i))
    return pl.pallas_call(
        kern,
        grid=(n // tt,),
        in_specs=[pl.BlockSpec((tt, d), lambda i: (i, 0)),
                  pl.BlockSpec(wk_t.shape, lambda i: (0, 0))],
        out_specs=[ospec, ospec, ospec, ospec],
        out_shape=[out(BF16), out(BF16), out(BF16), out(BF16)],
        scratch_shapes=[pltpu.VMEM((hp, nk, tt), F32)],
        compiler_params=_cparams(("parallel",)),
        name="route",
    )(x1b, wk_t)


PEER_DOT_SPLITS = (2, 2) + (4,) * 7
PEER_CHUNK_I1 = sum(PEER_DOT_SPLITS)


def _peer_kernel(xb_ref, x_ref, u_ref, vt_ref, cnt_ref, p1_ref, r2_ref, p2_ref, g_ref, b_ref,
                 out_ref, acc_ref, h_ref, *, heads, nk, alpha):
    j = pl.program_id(1)
    tt = xb_ref.shape[0]
    n_i1 = u_ref.shape[0] // nk
    pk = BF16_SUBLANES
    tiles = nk // pk

    @pl.when(j == 0)
    def _():
        acc_ref[...] = jnp.zeros_like(acc_ref)

    cnt_t = [cnt_ref[h].astype(F32) for h in range(heads)]
    p1_t = [p1_ref[h].astype(F32) for h in range(heads)]

    def expert_dots(i0, n):
        h = _dot_nt(u_ref[i0 * nk:(i0 + n) * nk, :], xb_ref[...])
        h_ref[i0 * tiles:(i0 + n) * tiles] = h.astype(BF16).reshape(n * tiles, pk, tt)

    def gate_slab(ii):
        w = jnp.zeros((tiles, pk, tt), BF16)
        for h in range(heads):
            cnt = jnp.broadcast_to(cnt_t[h][ii:ii + 1], (pk, tt)).astype(BF16)
            p1 = jnp.broadcast_to(p1_t[h][ii:ii + 1], (pk, tt)).astype(BF16)
            r2 = r2_ref[h].reshape(tiles, pk, tt)
            p2 = p2_ref[h].reshape(tiles, pk, tt)
            w = w + jnp.where(r2 < cnt[None], p2, 0.0) * p1[None]
        h_ref[ii * tiles:(ii + 1) * tiles] = w * _gelu_tanh(h_ref[ii * tiles:(ii + 1) * tiles])

    starts = [sum(PEER_DOT_SPLITS[:k]) for k in range(len(PEER_DOT_SPLITS))]
    expert_dots(0, PEER_DOT_SPLITS[0])
    for k, n in enumerate(PEER_DOT_SPLITS):
        if k + 1 < len(PEER_DOT_SPLITS):
            expert_dots(starts[k + 1], PEER_DOT_SPLITS[k + 1])
        for ii in range(starts[k], starts[k] + n):
            gate_slab(ii)
    acc_ref[...] += _dot(vt_ref[...], h_ref[...].reshape(n_i1 * nk, tt))

    @pl.when(j == pl.num_programs(1) - 1)
    def _():
        z = alpha * x_ref[...] + acc_ref[...].T
        out_ref[...] = _layer_norm_rows(z, g_ref[...], b_ref[...])


def _peer(x1, x1b, u_b, vt_b, routing, g_row, b_row, alpha, tt, chunk):
    n, d = x1.shape
    cnt, p1, r2, p2 = routing
    heads, nk, _ = cnt.shape
    n_exp = u_b.shape[0]
    assert chunk == PEER_CHUNK_I1 * nk and PEER_CHUNK_I1 % BF16_SUBLANES == 0, "whole packed tiles of i1 rows per step"
    kern = functools.partial(_peer_kernel, heads=heads, nk=nk, alpha=alpha)
    rspec = pl.BlockSpec((heads, nk, tt), lambda i, j: (0, 0, i))
    cspec = pl.BlockSpec((heads, PEER_CHUNK_I1, tt), lambda i, j: (0, j, i))
    packed = (chunk // BF16_SUBLANES, BF16_SUBLANES, tt)
    return pl.pallas_call(
        kern,
        grid=(n // tt, n_exp // chunk),
        in_specs=[pl.BlockSpec((tt, d), lambda i, j: (i, 0)),
                  pl.BlockSpec((tt, d), lambda i, j: (i, 0)),
                  pl.BlockSpec((chunk, d), lambda i, j: (j, 0)),
                  pl.BlockSpec((d, chunk), lambda i, j: (0, j)),
                  cspec, cspec, rspec, rspec,
                  pl.BlockSpec((1, d), lambda i, j: (0, 0)),
                  pl.BlockSpec((1, d), lambda i, j: (0, 0))],
        out_specs=pl.BlockSpec((tt, d), lambda i, j: (i, 0)),
        out_shape=jax.ShapeDtypeStruct((n, d), F32),
        scratch_shapes=[pltpu.VMEM((d, tt), F32), pltpu.VMEM(packed, BF16)],
        compiler_params=_cparams(("parallel", "arbitrary")),
        name="peer",
    )(x1b, x1, u_b, vt_b, cnt, p1, r2, p2, g_row, b_row)


def _stage2(x1, x1b, w_pq, sub_keys, peer_u, peer_v, ln2_g, ln2_b, alpha):
    n, d = x1.shape
    heads, _, nk, half = sub_keys.shape
    tt = min(512, n)
    wq = w_pq.astype(F32).reshape(d, heads * 2, half)
    wk_t = jnp.einsum('dhk,hnk->hnd', wq, sub_keys.astype(F32).reshape(heads * 2, nk, half),
                      precision=lax.Precision.HIGHEST).reshape(heads * 2 * nk, d).astype(BF16)
    routing = _route(x1b, wk_t, heads, nk, min(1024, n))
    return _peer(x1, x1b, peer_u.astype(BF16), peer_v.T.astype(BF16), routing,
                 ln2_g.astype(F32).reshape(1, d), ln2_b.astype(F32).reshape(1, d), alpha, tt, chunk=PEER_CHUNK_I1 * nk)


def kernel(x, w_in, s5_lam_re, s5_lam_im, s5_b_re, s5_b_im, s5_c_re, s5_c_im, s5_d, s5_log_step, s5_w_glu,
           s5_b_glu, hgrn_lb_logits, hgrn_gnorm_w, w_pa, w_pb, w_o, ln1_g, ln1_b, peer_w_q, peer_sub_keys,
           peer_u, peer_v, ln2_g, ln2_b):
    depth = w_in.shape[0]
    alpha = (2.0 * depth) ** 0.25
    lb_all = jnp.cumsum(jax.nn.softmax(hgrn_lb_logits.astype(F32), axis=0), axis=0)[:depth]
    bsz, seq, d = x.shape
    for l in range(depth):
        x1, x1b = _stage1(x, w_in[l], s5_lam_re[l], s5_lam_im[l], s5_b_re[l], s5_b_im[l], s5_c_re[l], s5_c_im[l],
                          s5_d[l], s5_log_step[l], s5_w_glu[l], s5_b_glu[l], lb_all[l], hgrn_gnorm_w[l],
                          w_pa[l], w_pb[l], w_o[l], ln1_g[l], ln1_b[l], alpha)
        x = _stage2(x1, x1b, peer_w_q[l], peer_sub_keys[l], peer_u[l], peer_v[l], ln2_g[l], ln2_b[l],
                    alpha).reshape(bsz, seq, d)
    return x
```

```python
import functools
import math

import numpy as np
import jax
import jax.numpy as jnp
from jax import lax
from jax.experimental import pallas as pl
from jax.experimental.pallas import tpu as pltpu

F32 = jnp.float32
BF16 = jnp.bfloat16

V7X_VMEM_LIMIT_BYTES = 58 * 1024 * 1024
BF16_SUBLANES = 16

S5_GROUPS_PER_BLOCK = 8
HGRN_HEAD_DIM = 128
HGRN_CHUNK = 128
PEER_TOPK = 16
LN_EPS = 1e-5
RMS_EPS = 1e-6


def _cparams(semantics):
    return pltpu.CompilerParams(dimension_semantics=semantics, vmem_limit_bytes=V7X_VMEM_LIMIT_BYTES)


def _dot(a, b):
    return jnp.dot(a, b, preferred_element_type=F32)


def _dot_nt(a, b):
    return lax.dot_general(a, b, (((1,), (1,)), ((), ())), preferred_element_type=F32)


def _dot_tn(a, b):
    return lax.dot_general(a, b, (((0,), (0,)), ((), ())), preferred_element_type=F32)


def _sigmoid(x):
    return 1.0 / (1.0 + jnp.exp(-x))


def _gelu_tanh(x):
    c = math.sqrt(2.0 / math.pi)
    return 0.5 * x * (1.0 + jnp.tanh(c * (x + 0.044715 * (x * x * x))))


def _layer_norm_rows(z, g, b):
    mu = jnp.mean(z, axis=-1, keepdims=True)
    zc = z - mu
    var = jnp.mean(zc * zc, axis=-1, keepdims=True)
    return zc * lax.rsqrt(var + LN_EPS) * g + b


def _proj_kernel(x_ref, w_ref, u_ref, hg_ref, gate_ref, *, s5_w, hg_w, col_chunk):
    xb = x_ref[...].astype(BF16)
    n_cols = w_ref.shape[1]
    for c0 in range(0, n_cols, col_chunk):
        p = _dot(xb, w_ref[:, c0:c0 + col_chunk]).astype(BF16)
        if c0 < s5_w:
            u_ref[:, c0:c0 + col_chunk] = p
        elif c0 < s5_w + hg_w:
            hg_ref[:, c0 - s5_w:c0 - s5_w + col_chunk] = p
        else:
            o = c0 - s5_w - hg_w
            gate_ref[:, o:o + col_chunk] = p


def _proj(x2, w_in_b, bsz, seq, s5_w, hg_w, gate_w, tm):
    n, d = x2.shape
    n_s = seq // tm
    kern = functools.partial(_proj_kernel, s5_w=s5_w, hg_w=hg_w, col_chunk=s5_w)
    return pl.pallas_call(
        kern,
        grid=(bsz, n_s),
        in_specs=[
            pl.BlockSpec((tm, d), lambda b, i: (b * n_s + i, 0)),
            pl.BlockSpec(w_in_b.shape, lambda b, i: (0, 0)),
        ],
        out_specs=[
            pl.BlockSpec((tm, s5_w), lambda b, i: (i, b)),
            pl.BlockSpec((tm, hg_w), lambda b, i: (b * n_s + i, 0)),
            pl.BlockSpec((tm, gate_w), lambda b, i: (b * n_s + i, 0)),
        ],
        out_shape=[
            jax.ShapeDtypeStruct((seq, bsz * s5_w), BF16),
            jax.ShapeDtypeStruct((n, hg_w), BF16),
            jax.ShapeDtypeStruct((n, gate_w), BF16),
        ],
        compiler_params=_cparams(("parallel", "parallel")),
        name="proj",
    )(x2, w_in_b)


def _s5_kernel(u_ref, bblk_ref, cblk_ref, lamr_ref, lami_ref, d_ref, wglu_ref, bglu_ref,
               y_ref, xs_ref, st_ref, *, bsz, tb, n_blk, half):
    @pl.when(pl.program_id(0) == 0)
    def _():
        st_ref[...] = jnp.zeros_like(st_ref)

    in_w = bblk_ref.shape[1]
    ys = []
    for j in range(n_blk):
        uj = u_ref[:, j * in_w:(j + 1) * in_w]
        xs_ref[...] = _dot(uj, bblk_ref[j])
        lr = jnp.broadcast_to(lamr_ref[j], (bsz, half))
        li = jnp.broadcast_to(lami_ref[j], (bsz, half))

        def body(t, carry, lr=lr, li=li):
            sr, si = carry
            rows = pl.ds(pl.multiple_of(t * bsz, bsz), bsz)
            nr = lr * sr - li * si + xs_ref[rows, :half]
            ni = lr * si + li * sr + xs_ref[rows, half:]
            xs_ref[rows, :half] = nr
            xs_ref[rows, half:] = ni
            return nr, ni

        sr, si = lax.fori_loop(0, tb, body, (st_ref[j, :, :half], st_ref[j, :, half:]), unroll=True)
        st_ref[j, :, :half] = sr
        st_ref[j, :, half:] = si
        ys.append(_dot(xs_ref[...].astype(BF16), cblk_ref[j]))
    y = jnp.concatenate(ys, axis=1) + d_ref[...] * u_ref[...].astype(F32)
    y = _gelu_tanh(y)
    z = _dot(y.astype(BF16), wglu_ref[...]) + bglu_ref[...]
    y_ref[...] = (y * _sigmoid(z)).astype(BF16)


def _s5_params(lam_re, lam_im, b_re, b_im, c_re, c_im, d_skip, log_step):
    g, p = lam_re.shape
    h = b_re.shape[-1]
    gb = S5_GROUPS_PER_BLOCK
    nb = g // gb
    lr, li = lam_re.astype(F32), lam_im.astype(F32)
    dt = jnp.exp(log_step.astype(F32))[:, None]
    mag = jnp.exp(lr * dt)
    lbr, lbi = mag * jnp.cos(li * dt), mag * jnp.sin(li * dt)
    den = lr * lr + li * li
    cr = ((lbr - 1.0) * lr + lbi * li) / den
    ci = (lbi * lr - (lbr - 1.0) * li) / den
    br, bi = b_re.astype(F32), b_im.astype(F32)
    bbar_r = cr[..., None] * br - ci[..., None] * bi
    bbar_i = cr[..., None] * bi + ci[..., None] * br
    eye = jnp.eye(gb, dtype=F32)

    def in_blk(m):
        m = m.reshape(nb, gb, p, h)
        return jnp.einsum('jgph,gk->jghkp', m, eye).reshape(nb, gb * h, gb * p)

    def out_blk(m):
        m = m.reshape(nb, gb, h, p)
        return jnp.einsum('jghp,gk->jgpkh', m, eye).reshape(nb, gb * p, gb * h)

    bblk = jnp.concatenate([in_blk(bbar_r), in_blk(bbar_i)], axis=2).astype(BF16)
    cblk = jnp.concatenate([out_blk(c_re.astype(F32)), out_blk(-c_im.astype(F32))], axis=1).astype(BF16)
    lamr = lbr.reshape(nb, 1, gb * p)
    lami = lbi.reshape(nb, 1, gb * p)
    return bblk, cblk, lamr, lami, d_skip.astype(F32).reshape(1, g * h)


def _s5(u_tm, params, w_glu_b, b_glu, bsz, seq, tb):
    bblk, cblk, lamr, lami, d_row = params
    n, w = u_tm.shape
    n_blk, in_w, two_half = bblk.shape
    half = two_half // 2
    rows = tb * bsz
    kern = functools.partial(_s5_kernel, bsz=bsz, tb=tb, n_blk=n_blk, half=half)
    const = lambda a: pl.BlockSpec(a.shape, lambda i: (0,) * a.ndim)
    return pl.pallas_call(
        kern,
        grid=(seq // tb,),
        in_specs=[pl.BlockSpec((rows, w), lambda i: (i, 0)),
                  const(bblk), const(cblk), const(lamr), const(lami), const(d_row),
                  const(w_glu_b), const(b_glu)],
        out_specs=pl.BlockSpec((rows, w), lambda i: (i, 0)),
        out_shape=jax.ShapeDtypeStruct((n, w), BF16),
        scratch_shapes=[pltpu.VMEM((rows, two_half), F32), pltpu.VMEM((n_blk, bsz, two_half), F32)],
        compiler_params=_cparams(("arbitrary",)),
        name="s5",
    )(u_tm, bblk, cblk, lamr, lami, d_row, w_glu_b, b_glu)


def _hgrn_tables(c):
    n_lev = int(math.log2(c))
    t = np.arange(c)[:, None]
    r = np.arange(c)[None, :]
    sel = []
    masks = []
    for lev in range(n_lev):
        w = c >> (lev + 1)
        pos = t % (2 * w)
        a = t - pos + w - 1
        upper = pos >= w
        sel.append(np.where(upper, (r > a) & (r <= t), (r > t) & (r <= a)))
        tt, ss = np.arange(c)[:, None], np.arange(c)[None, :]
        masks.append((tt // (2 * w) == ss // (2 * w)) & (tt % (2 * w) >= w) & (ss % (2 * w) < w))
    sel.append(r <= t)
    sel.append(r > t)
    masks.append(np.eye(c, dtype=bool))
    sel = np.concatenate(sel, axis=0).astype(np.float32)
    sel = np.concatenate([sel, sel], axis=1)
    masks = np.stack(masks, axis=0).astype(np.float32)
    return jnp.asarray(sel, BF16), jnp.asarray(masks, F32)


def _hgrn_kernel(hg_ref, sel_ref, mask_ref, lb_ref, gw_ref, y_ref, st_ref, *, c, n_chunks, width, heads):
    @pl.when(pl.program_id(1) == 0)
    def _():
        st_ref[...] = jnp.zeros_like(st_ref)

    n_lev = mask_ref.shape[0] - 1
    dh = width // heads
    lb = lb_ref[...]
    gw = gw_ref[...]

    def chunk(ci, carry):
        rows = pl.ds(pl.multiple_of(ci * c, c), c)
        q = hg_ref[0, rows, 0:width].astype(F32)
        fp = hg_ref[0, rows, width:2 * width].astype(F32)
        v = hg_ref[0, rows, 2 * width:3 * width]
        g = hg_ref[0, rows, 3 * width:4 * width].astype(F32)
        f = lb + (1.0 - lb) * _sigmoid(fp)
        lf = jnp.log(f)
        k = 1.0 - f
        qf = q * _sigmoid(q)
        h1 = lf.astype(BF16)
        h2 = (lf - h1.astype(F32)).astype(BF16)
        e_all = jnp.exp(_dot(sel_ref[...], jnp.concatenate([h1, h2], axis=0)))
        e_cum = e_all[n_lev * c:(n_lev + 1) * c]
        e_suf = e_all[(n_lev + 1) * c:(n_lev + 2) * c]
        outs = []
        for hd in range(heads):
            cols = slice(hd * dh, (hd + 1) * dh)
            qh, kh = qf[:, cols], k[:, cols]
            vh = v[:, cols]
            sc = mask_ref[n_lev] * _dot_nt(qh.astype(BF16), kh.astype(BF16))
            for lev in range(n_lev):
                el = e_all[lev * c:(lev + 1) * c, cols]
                sc = sc + mask_ref[lev] * _dot_nt((qh * el).astype(BF16), (kh * el).astype(BF16))
            st = st_ref[hd]
            o = _dot_nt((qh * e_cum[:, cols]).astype(BF16), st.astype(BF16)) + _dot(sc.astype(BF16), vh)
            e_tot = e_cum[c - 1:c, cols]
            st_ref[hd] = st * e_tot + _dot_tn(vh, (kh * e_suf[:, cols]).astype(BF16))
            o = o * lax.rsqrt(jnp.mean(o * o, axis=-1, keepdims=True) + RMS_EPS)
            outs.append(o)
        o = jnp.concatenate(outs, axis=1) * gw * (g * _sigmoid(g))
        y_ref[0, rows, :] = o.astype(BF16)
        return carry

    lax.fori_loop(0, n_chunks, chunk, 0, unroll=True)


def _hgrn(hg3, lb_row, gw_row, heads, tbh):
    bsz, seq, w4 = hg3.shape
    width = w4 // 4
    c = HGRN_CHUNK
    sel, masks = _hgrn_tables(c)
    kern = functools.partial(_hgrn_kernel, c=c, n_chunks=tbh // c, width=width, heads=heads)
    const = lambda a: pl.BlockSpec(a.shape, lambda b, i: (0,) * a.ndim)
    return pl.pallas_call(
        kern,
        grid=(bsz, seq // tbh),
        in_specs=[pl.BlockSpec((1, tbh, w4), lambda b, i: (b, i, 0)),
                  const(sel), const(masks), const(lb_row), const(gw_row)],
        out_specs=pl.BlockSpec((1, tbh, width), lambda b, i: (b, i, 0)),
        out_shape=jax.ShapeDtypeStruct((bsz, seq, width), BF16),
        scratch_shapes=[pltpu.VMEM((heads, width // heads, width // heads), F32)],
        compiler_params=_cparams(("parallel", "arbitrary")),
        name="hgrn",
    )(hg3, sel, masks, lb_row, gw_row)


def _merge_kernel(x_ref, ya_ref, yb_ref, gate_ref, wpa_ref, wpb_ref, wo_ref, g_ref, b_ref,
                  x1_ref, x1b_ref, *, alpha, d):
    ga = gate_ref[:, :d].astype(F32)
    gb = gate_ref[:, d:].astype(F32)
    merged = _sigmoid(ga) * _dot(ya_ref[...], wpa_ref[...]) + _sigmoid(gb) * _dot(yb_ref[...], wpb_ref[...])
    mix = _dot(merged.astype(BF16), wo_ref[...])
    x1 = _layer_norm_rows(alpha * x_ref[...] + mix, g_ref[...], b_ref[...])
    x1_ref[...] = x1
    x1b_ref[...] = x1.astype(BF16)


def _merge(x2, ya_tm, yb2, gates, wpa_b, wpb_b, wo_b, g_row, b_row, bsz, seq, alpha, tm):
    n, d = x2.shape
    w = yb2.shape[1]
    n_s = seq // tm
    kern = functools.partial(_merge_kernel, alpha=alpha, d=d)
    const = lambda a: pl.BlockSpec(a.shape, lambda b, i: (0,) * a.ndim)
    row = lambda cols: pl.BlockSpec((tm, cols), lambda b, i: (b * n_s + i, 0))
    return pl.pallas_call(
        kern,
        grid=(bsz, n_s),
        in_specs=[row(d),
                  pl.BlockSpec((tm, w), lambda b, i: (i, b)),
                  row(w), row(2 * d),
                  const(wpa_b), const(wpb_b), const(wo_b), const(g_row), const(b_row)],
        out_specs=[row(d), row(d)],
        out_shape=[jax.ShapeDtypeStruct((n, d), F32), jax.ShapeDtypeStruct((n, d), BF16)],
        compiler_params=_cparams(("parallel", "parallel")),
        name="merge",
    )(x2, ya_tm, yb2, gates, wpa_b, wpb_b, wo_b, g_row, b_row)


def _stage1(x, w_in, s5_lam_re, s5_lam_im, s5_b_re, s5_b_im, s5_c_re, s5_c_im, s5_d, s5_log_step,
            s5_w_glu, s5_b_glu, lb, gnorm_w, w_pa, w_pb, w_o, ln1_g, ln1_b, alpha):
    bsz, seq, d = x.shape
    n = bsz * seq
    s5_w = s5_w_glu.shape[0]
    hg_w = 4 * gnorm_w.shape[0]
    gate_w = 2 * d
    tm = min(512, seq)
    x2 = x.reshape(n, d)
    u_tm, hg, gates = _proj(x2, w_in.astype(BF16), bsz, seq, s5_w, hg_w, gate_w, tm)
    s5p = _s5_params(s5_lam_re, s5_lam_im, s5_b_re, s5_b_im, s5_c_re, s5_c_im, s5_d, s5_log_step)
    ya = _s5(u_tm.reshape(seq * bsz, s5_w), s5p, s5_w_glu.astype(BF16),
             s5_b_glu.astype(F32).reshape(1, s5_w), bsz, seq, tb=min(128, seq))
    yb = _hgrn(hg.reshape(bsz, seq, hg_w), lb.reshape(1, -1), gnorm_w.astype(F32).reshape(1, -1),
               heads=gnorm_w.shape[0] // HGRN_HEAD_DIM, tbh=min(1024, seq))
    return _merge(x2, ya.reshape(seq, bsz * s5_w), yb.reshape(n, -1), gates,
                  w_pa.astype(BF16), w_pb.astype(BF16), w_o.astype(BF16),
                  ln1_g.astype(F32).reshape(1, d), ln1_b.astype(F32).reshape(1, d), bsz, seq, alpha, tm)


RANK_BASE = 2.0 ** 100


def _top_sorted(s, k):
    rows = []
    cur = s
    for j in range(k):
        mx = jnp.max(cur, axis=0, keepdims=True)
        rows.append(mx)
        cur = jnp.where(cur == mx, -(j + 1) * RANK_BASE, cur)
    rank = jnp.where(cur < -0.5 * RANK_BASE, cur * (-1.0 / RANK_BASE) - 1.0, float(k))
    return rows, rank


def _route_kernel(xb_ref, wk_ref, cnt_ref, p1_ref, r2_ref, p2_ref, s_scr, *, heads, topk):
    s_scr[...] = _dot_nt(wk_ref[...], xb_ref[...]).reshape(s_scr.shape)

    def head(h, carry):
        s1 = s_scr[2 * h]
        s2 = s_scr[2 * h + 1]
        a, rank1 = _top_sorted(s1, topk)
        b, rank2 = _top_sorted(s2, topk)
        a_all = jnp.concatenate(a, axis=0)
        b_all = jnp.concatenate(b, axis=0)
        half_rows = topk // 2
        j_idx = lax.broadcasted_iota(jnp.int32, (half_rows, 1), 0)
        cands = [a_all + b[0]]
        for l in range(1, half_rows):
            cands.append(jnp.where(j_idx < topk // (l + 1), a_all[:half_rows] + b[l], -jnp.inf))
        cands.append(a[0] + b_all[half_rows:])
        cur = jnp.concatenate(cands, axis=0)
        m = a[0] + b[0]
        z = jnp.zeros_like(m)
        tau = m
        for r in range(topk):
            tau = jnp.max(cur, axis=0, keepdims=True)
            z = z + jnp.exp(tau - m)
            if r + 1 < topk:
                cur = jnp.where(cur == tau, -jnp.inf, cur)
        hit = lambda c: jnp.where(c >= tau, 1.0, 0.0)
        cnt_lo = hit(cands[0][:half_rows])
        for l in range(1, half_rows):
            cnt_lo = cnt_lo + hit(cands[l])
        cnt_0 = cnt_lo[0:1] + jnp.sum(hit(cands[half_rows]), axis=0, keepdims=True)
        cnt_rows = jnp.concatenate([cnt_lo, hit(cands[0][half_rows:])], axis=0)
        cnt = jnp.zeros(s1.shape, F32)
        for c in range(half_rows):
            ranks_with = jnp.sum(jnp.where(cnt_rows > float(c), 1.0, 0.0), axis=0, keepdims=True)
            cnt = jnp.where(rank1 < ranks_with, float(c + 1), cnt)
        cnt = jnp.where(rank1 == 0.0, jnp.maximum(cnt, cnt_0), cnt)
        cnt_ref[h] = cnt.astype(BF16)
        p1_ref[h] = jnp.exp(s1 - a[0]).astype(BF16)
        r2_ref[h] = rank2.astype(BF16)
        p2_ref[h] = (jnp.exp(s2 - b[0]) / z).astype(BF16)
        return carry

    lax.fori_loop(0, heads, head, 0)


def _route(x1b, wk_t, heads, nk, tt):
    n, d = x1b.shape
    hp = 2 * heads
    kern = functools.partial(_route_kernel, heads=heads, topk=PEER_TOPK)
    out = lambda dt: jax.ShapeDtypeStruct((heads, nk, n), dt)
    ospec = pl.BlockSpec((heads, nk, tt), lambda i: (0, 0, i))
    return pl.pallas_call(
        kern,
        grid=(n // tt,),
        in_specs=[pl.BlockSpec((tt, d), lambda i: (i, 0)),
                  pl.BlockSpec(wk_t.shape, lambda i: (0, 0))],
        out_specs=[ospec, ospec, ospec, ospec],
        out_shape=[out(BF16), out(BF16), out(BF16), out(BF16)],
        scratch_shapes=[pltpu.VMEM((hp, nk, tt), F32)],
        compiler_params=_cparams(("parallel",)),
        name="route",
    )(x1b, wk_t)


PEER_DOT_SPLITS = (2, 2) + (4,) * 7
PEER_CHUNK_I1 = sum(PEER_DOT_SPLITS)


def _peer_kernel(xb_ref, x_ref, u_ref, vt_ref, cnt_ref, p1_ref, r2_ref, p2_ref, g_ref, b_ref,
                 out_ref, acc_ref, h_ref, *, heads, nk, alpha):
    j = pl.program_id(1)
    tt = xb_ref.shape[0]
    n_i1 = u_ref.shape[0] // nk
    pk = BF16_SUBLANES
    tiles = nk // pk

    @pl.when(j == 0)
    def _():
        acc_ref[...] = jnp.zeros_like(acc_ref)


    def row(ref, h, ii):
        t0 = (ii // pk) * pk
        return ref[h, t0:t0 + pk, :].astype(F32)[ii - t0:ii - t0 + 1]

    def expert_dots(i0, n):
        h = _dot_nt(u_ref[i0 * nk:(i0 + n) * nk, :], xb_ref[...])
        h_ref[i0 * tiles:(i0 + n) * tiles] = h.astype(BF16).reshape(n * tiles, pk, tt)

    def gate_slab(ii):
        w = jnp.zeros((tiles, pk, tt), BF16)
        for h in range(heads):
            cnt = jnp.broadcast_to(row(cnt_ref, h, ii), (pk, tt)).astype(BF16)
            p1 = jnp.broadcast_to(row(p1_ref, h, ii), (pk, tt)).astype(BF16)
            r2 = r2_ref[h].reshape(tiles, pk, tt)
            p2 = p2_ref[h].reshape(tiles, pk, tt)
            w = w + jnp.where(r2 < cnt[None], p2, 0.0) * p1[None]
        h_ref[ii * tiles:(ii + 1) * tiles] = w * _gelu_tanh(h_ref[ii * tiles:(ii + 1) * tiles])

    starts = [sum(PEER_DOT_SPLITS[:k]) for k in range(len(PEER_DOT_SPLITS))]
    expert_dots(0, PEER_DOT_SPLITS[0])
    for k, n in enumerate(PEER_DOT_SPLITS):
        if k + 1 < len(PEER_DOT_SPLITS):
            expert_dots(starts[k + 1], PEER_DOT_SPLITS[k + 1])
        for ii in range(starts[k], starts[k] + n):
            gate_slab(ii)
    acc_ref[...] += _dot(vt_ref[...], h_ref[...].reshape(n_i1 * nk, tt))

    @pl.when(j == pl.num_programs(1) - 1)
    def _():
        z = alpha * x_ref[...] + acc_ref[...].T
        out_ref[...] = _layer_norm_rows(z, g_ref[...], b_ref[...])


def _peer(x1, x1b, u_b, vt_b, routing, g_row, b_row, alpha, tt, chunk):
    n, d = x1.shape
    cnt, p1, r2, p2 = routing
    heads, nk, _ = cnt.shape
    n_exp = u_b.shape[0]
    assert chunk == PEER_CHUNK_I1 * nk and PEER_CHUNK_I1 % BF16_SUBLANES == 0, "whole packed tiles of i1 rows per step"
    kern = functools.partial(_peer_kernel, heads=heads, nk=nk, alpha=alpha)
    rspec = pl.BlockSpec((heads, nk, tt), lambda i, j: (0, 0, i))
    cspec = pl.BlockSpec((heads, PEER_CHUNK_I1, tt), lambda i, j: (0, j, i))
    packed = (chunk // BF16_SUBLANES, BF16_SUBLANES, tt)
    return pl.pallas_call(
        kern,
        grid=(n // tt, n_exp // chunk),
        in_specs=[pl.BlockSpec((tt, d), lambda i, j: (i, 0)),
                  pl.BlockSpec((tt, d), lambda i, j: (i, 0)),
                  pl.BlockSpec((chunk, d), lambda i, j: (j, 0)),
                  pl.BlockSpec((d, chunk), lambda i, j: (0, j)),
                  cspec, cspec, rspec, rspec,
                  pl.BlockSpec((1, d), lambda i, j: (0, 0)),
                  pl.BlockSpec((1, d), lambda i, j: (0, 0))],
        out_specs=pl.BlockSpec((tt, d), lambda i, j: (i, 0)),
        out_shape=jax.ShapeDtypeStruct((n, d), F32),
        scratch_shapes=[pltpu.VMEM((d, tt), F32), pltpu.VMEM(packed, BF16)],
        compiler_params=_cparams(("parallel", "arbitrary")),
        name="peer",
    )(x1b, x1, u_b, vt_b, cnt, p1, r2, p2, g_row, b_row)


def _stage2(x1, x1b, w_pq, sub_keys, peer_u, peer_v, ln2_g, ln2_b, alpha):
    n, d = x1.shape
    heads, _, nk, half = sub_keys.shape
    tt = min(512, n)
    wq = w_pq.astype(F32).reshape(d, heads * 2, half)
    wk_t = jnp.einsum('dhk,hnk->hnd', wq, sub_keys.astype(F32).reshape(heads * 2, nk, half),
                      precision=lax.Precision.HIGHEST).reshape(heads * 2 * nk, d).astype(BF16)
    routing = _route(x1b, wk_t, heads, nk, min(1024, n))
    return _peer(x1, x1b, peer_u.astype(BF16), peer_v.T.astype(BF16), routing,
                 ln2_g.astype(F32).reshape(1, d), ln2_b.astype(F32).reshape(1, d), alpha, tt, chunk=PEER_CHUNK_I1 * nk)


def kernel(x, w_in, s5_lam_re, s5_lam_im, s5_b_re, s5_b_im, s5_c_re, s5_c_im, s5_d, s5_log_step, s5_w_glu,
           s5_b_glu, hgrn_lb_logits, hgrn_gnorm_w, w_pa, w_pb, w_o, ln1_g, ln1_b, peer_w_q, peer_sub_keys,
           peer_u, peer_v, ln2_g, ln2_b):
    depth = w_in.shape[0]
    alpha = (2.0 * depth) ** 0.25
    lb_all = jnp.cumsum(jax.nn.softmax(hgrn_lb_logits.astype(F32), axis=0), axis=0)[:depth]
    bsz, seq, d = x.shape
    for l in range(depth):
        x1, x1b = _stage1(x, w_in[l], s5_lam_re[l], s5_lam_im[l], s5_b_re[l], s5_b_im[l], s5_c_re[l], s5_c_im[l],
                          s5_d[l], s5_log_step[l], s5_w_glu[l], s5_b_glu[l], lb_all[l], hgrn_gnorm_w[l],
                          w_pa[l], w_pb[l], w_o[l], ln1_g[l], ln1_b[l], alpha)
        x = _stage2(x1, x1b, peer_w_q[l], peer_sub_keys[l], peer_u[l], peer_v[l], ln2_g[l], ln2_b[l],
                    alpha).reshape(bsz, seq, d)
    return x
```
